```python
import math
import jax, jax.numpy as jnp
from jax import lax
import numpy as np


D_MODEL = 4096
BATCH = 16
SEQ = 256
DEPTH = 2
DEC_BATCH = 2
DEC_SEQ = 4096
PAST_LEN = 256

GRID_W = 64
HD_A = 128
H_A = D_MODEL // (4 * HD_A)
HD_B = 128
H_B = D_MODEL // (2 * HD_B)
A_QK = 2 * H_A * HD_A
A_V = H_A * 2 * HD_A
B_W = H_B * HD_B
EVEN_IN = 2 * A_QK + A_V + 3 * B_W
EVEN_MIX = A_V + B_W
NA_ROWS = 8
NA_COLS = 16
ROPE_BASE = 10000.0
Q_BLOCK = 128
CHUNK = 128
G_C = 8
D_C = D_MODEL
D_FF = ((8 * D_MODEL // 3 + 255) // 256) * 256
CONV_W = 3
N_EVEN = (DEPTH + 1) // 2
N_ODD = DEPTH // 2
NORM_EPS = 1e-6

kernel_name = 'hybrid_diffattn_natten_chunkmlp_prefix_step'


def rms_norm(x, g):
    xf = x.astype(jnp.float32)
    y = xf * lax.rsqrt(jnp.mean(xf * xf, axis=-1, keepdims=True) + NORM_EPS)
    return (y * g.astype(jnp.float32)).astype(x.dtype)


def adaln(cond, w, b):
    return (jax.nn.silu(cond) @ w + b)[:, None, :]


def modulate(h, shift, scale):
    return h * (1 + scale) + shift


def axial_rope(x):
    n_tok, d = x.shape[1], x.shape[-1]
    quarter = d // 4
    t = jnp.arange(n_tok)
    inv = ROPE_BASE ** (-jnp.arange(quarter, dtype=jnp.float32) / quarter)

    def ang(p):
        a = p.astype(jnp.float32)[:, None] * inv[None, :]
        return jnp.concatenate([a, a], axis=-1)

    angles = jnp.concatenate([ang(t // GRID_W), ang(t % GRID_W)], axis=-1)
    bshape = (1, n_tok) + (1,) * (x.ndim - 3) + (d,)
    cos = jnp.cos(angles).reshape(bshape)
    sin = jnp.sin(angles).reshape(bshape)
    xf = x.astype(jnp.float32)
    xs = xf.reshape(xf.shape[:-1] + (2, 2, quarter))
    rot = jnp.concatenate([-xs[..., 1:, :], xs[..., :1, :]], axis=-2).reshape(xf.shape)
    return (xf * cos + rot * sin).astype(x.dtype)


def diff_attention(q, k, v, lam):
    bsz, lq = q.shape[:2]
    scale = q.shape[-1] ** -0.5
    qb = jnp.moveaxis(q.reshape((bsz, lq // Q_BLOCK, Q_BLOCK) + q.shape[2:]), 1, 0)

    def blk(qi):
        s = jnp.einsum('bqjhd,bkjhd->bjhqk', qi, k, preferred_element_type=jnp.float32) * scale
        p = jax.nn.softmax(s, axis=-1)
        a = p[:, 0] - lam * p[:, 1]
        return jnp.einsum('bhqk,bkhe->bqhe', a.astype(v.dtype), v)

    o = lax.map(blk, qb)
    return jnp.moveaxis(o, 0, 1).reshape((bsz, lq) + v.shape[2:])


def softmax_attention(q, k, v):
    bsz, lq = q.shape[:2]
    scale = q.shape[-1] ** -0.5
    qb = jnp.moveaxis(q.reshape((bsz, lq // Q_BLOCK, Q_BLOCK) + q.shape[2:]), 1, 0)

    def blk(qi):
        s = jnp.einsum('bqhd,bkhd->bhqk', qi, k, preferred_element_type=jnp.float32) * scale
        p = jax.nn.softmax(s, axis=-1)
        return jnp.einsum('bhqk,bkhd->bqhd', p.astype(v.dtype), v)

    o = lax.map(blk, qb)
    return jnp.moveaxis(o, 0, 1).reshape((bsz, lq) + v.shape[2:])


def neighborhood_attention(q, k, v, k_ctx, v_ctx, rpb):
    bsz, n_tok, n_h, hd = q.shape
    rows = n_tok // GRID_W
    wr = min(NA_ROWS, rows)
    wc = NA_COLS
    scale = hd ** -0.5
    qg = q.reshape(bsz, rows, GRID_W, n_h, hd)
    kg = k.reshape(bsz, rows, GRID_W, n_h, hd)
    vg = v.reshape(bsz, rows, GRID_W, n_h, hd)
    cols = jnp.arange(GRID_W)
    cstart = jnp.clip(cols - wc // 2, 0, GRID_W - wc)
    col_mask = (cols[None, :] >= cstart[:, None]) & (cols[None, :] < cstart[:, None] + wc)
    col_idx = jnp.clip(cols[None, :] - cols[:, None] + NA_COLS - 1, 0, 2 * NA_COLS - 2)
    n_band = wr * GRID_W

    def one_row(r):
        rs = jnp.clip(r - wr // 2, 0, rows - wr)
        q_r = lax.dynamic_index_in_dim(qg, r, axis=1, keepdims=False)
        k_band = lax.dynamic_slice_in_dim(kg, rs, wr, axis=1)
        v_band = lax.dynamic_slice_in_dim(vg, rs, wr, axis=1)
        row_idx = rs + jnp.arange(wr) - r + NA_ROWS - 1
        bias = rpb[:, row_idx[None, :, None], col_idx[:, None, :]]
        s_loc = jnp.einsum('bchd,bswhd->bhcsw', q_r, k_band, preferred_element_type=jnp.float32) * scale + bias
        s_loc = jnp.where(col_mask[:, None, :], s_loc, -jnp.inf)
        s_ctx = jnp.einsum('bchd,bkhd->bhck', q_r, k_ctx, preferred_element_type=jnp.float32) * scale
        s = jnp.concatenate([s_loc.reshape(bsz, n_h, GRID_W, n_band), s_ctx], axis=-1)
        p = jax.nn.softmax(s, axis=-1).astype(v.dtype)
        p_loc = p[..., :n_band].reshape(bsz, n_h, GRID_W, wr, GRID_W)
        p_ctx = p[..., n_band:]
        return (jnp.einsum('bhcsw,bswhd->bchd', p_loc, v_band)
                + jnp.einsum('bhck,bkhd->bchd', p_ctx, v_ctx))

    out = lax.map(one_row, jnp.arange(rows))
    return jnp.moveaxis(out, 0, 1).reshape(bsz, n_tok, n_h, hd)


def even_projections(h, w_in, qn_a, kn_a, qn_b, kn_b):
    bsz, n_tok, _ = h.shape
    cuts = [A_QK, 2 * A_QK, 2 * A_QK + A_V, 2 * A_QK + A_V + B_W, 2 * A_QK + A_V + 2 * B_W]
    qa, ka, va, qb, kb, vb = jnp.split(h @ w_in, cuts, axis=-1)
    qa = rms_norm(qa.reshape(bsz, n_tok, 2, H_A, HD_A), qn_a)
    ka = rms_norm(ka.reshape(bsz, n_tok, 2, H_A, HD_A), kn_a)
    va = va.reshape(bsz, n_tok, H_A, 2 * HD_A)
    qb = rms_norm(qb.reshape(bsz, n_tok, H_B, HD_B), qn_b)
    kb = rms_norm(kb.reshape(bsz, n_tok, H_B, HD_B), kn_b)
    vb = vb.reshape(bsz, n_tok, H_B, HD_B)
    return qa, ka, va, qb, kb, vb


def even_output(o_a, o_b, subln_g, lam_init, w_out):
    bsz, n_tok = o_a.shape[:2]
    o_a = rms_norm(o_a, subln_g) * (1.0 - lam_init)
    o = jnp.concatenate([o_a.reshape(bsz, n_tok, A_V), o_b.reshape(bsz, n_tok, B_W)], axis=-1)
    return o @ w_out


def chunk_mlp(h, w_in, v_g, w_s, b_s, w_out):
    bsz, n_tok, _ = h.shape
    z = jax.nn.gelu(h @ w_in, approximate=False)
    u, v = jnp.split(z, 2, axis=-1)
    v = rms_norm(v, v_g).reshape(bsz, n_tok // CHUNK, CHUNK, G_C, D_C // G_C)
    sv = jnp.einsum('gpq,bnqgc->bnpgc', w_s, v) + b_s.T[:, :, None]
    return (u * sv.reshape(bsz, n_tok, D_C)) @ w_out


def conv_ffn(h, w_up, conv_w, conv_b, w_down):
    n_tok = h.shape[1]
    hu = h @ w_up
    pad = CONV_W // 2
    hp = jnp.pad(hu, ((0, 0), (pad, CONV_W - 1 - pad), (0, 0)))
    hc = sum(hp[:, j:j + n_tok] * conv_w[j] for j in range(CONV_W)) + conv_b
    g, u = jnp.split(hc, 2, axis=-1)
    return (jax.nn.silu(g) * u) @ w_down


def setup_inputs(seed: int = 0) -> dict:
    key = jax.random.key(seed)
    ks = iter(jax.random.split(key, 40))

    def nrm(shape, scale=1.0):
        return jax.random.normal(next(ks), shape, jnp.float32) * scale

    def gain(shape):
        return 1.0 + nrm(shape, 0.02)

    D = D_MODEL
    return {
        'x_prompt': nrm((BATCH, SEQ, D)),
        'x_sample': nrm((DEC_BATCH, DEC_SEQ, D)),
        'cache_a_k': nrm((DEC_BATCH, N_EVEN, PAST_LEN, 2, H_A, HD_A)),
        'cache_a_v': nrm((DEC_BATCH, N_EVEN, PAST_LEN, H_A, 2 * HD_A)),
        'cache_b_k': nrm((DEC_BATCH, N_EVEN, PAST_LEN, H_B, HD_B)),
        'cache_b_v': nrm((DEC_BATCH, N_EVEN, PAST_LEN, H_B, HD_B)),
        'c': nrm((DEC_BATCH, D)),
        'c_ctx': nrm((D,)),
        'w_ada': nrm((DEPTH, D, 6 * D), D ** -0.5),
        'b_ada': nrm((DEPTH, 6 * D), 0.02),
        'norm1_g': gain((DEPTH, D)),
        'norm2_g': gain((DEPTH, D)),
        'w_in_even': nrm((N_EVEN, D, EVEN_IN), D ** -0.5),
        'qn_a': gain((N_EVEN, HD_A)),
        'kn_a': gain((N_EVEN, HD_A)),
        'lam_q1': nrm((N_EVEN, HD_A), 0.1),
        'lam_k1': nrm((N_EVEN, HD_A), 0.1),
        'lam_q2': nrm((N_EVEN, HD_A), 0.1),
        'lam_k2': nrm((N_EVEN, HD_A), 0.1),
        'subln_g': gain((N_EVEN, 2 * HD_A)),
        'qn_b': gain((N_EVEN, HD_B)),
        'kn_b': gain((N_EVEN, HD_B)),
        'rpb': nrm((N_EVEN, H_B, 2 * NA_ROWS - 1, 2 * NA_COLS - 1), 0.02),
        'w_out_even': nrm((N_EVEN, EVEN_MIX, D), EVEN_MIX ** -0.5),
        'w_in_odd': nrm((N_ODD, D, 2 * D_C), D ** -0.5),
        'vnorm_g': gain((N_ODD, D_C)),
        'w_sp': nrm((N_ODD, G_C, CHUNK, CHUNK), CHUNK ** -0.5),
        'b_sp': nrm((N_ODD, G_C, CHUNK), 0.02),
        'w_out_odd': nrm((N_ODD, D_C, D), D_C ** -0.5),
        'w_up': nrm((DEPTH, D, 2 * D_FF), D ** -0.5),
        'conv_w': nrm((DEPTH, CONV_W, 2 * D_FF), CONV_W ** -0.5),
        'conv_b': nrm((DEPTH, 2 * D_FF), 0.02),
        'w_down': nrm((DEPTH, D_FF, D), D_FF ** -0.5),
    }


def reference(x_prompt, x_sample, cache_a_k, cache_a_v, cache_b_k, cache_b_v, c, c_ctx,
              w_ada, b_ada, norm1_g, norm2_g,
              w_in_even, qn_a, kn_a, lam_q1, lam_k1, lam_q2, lam_k2, subln_g, qn_b, kn_b, rpb, w_out_even,
              w_in_odd, vnorm_g, w_sp, b_sp, w_out_odd,
              w_up, conv_w, conv_b, w_down):
    xp, xs = x_prompt, x_sample
    sak, sav, sbk, sbv = [], [], [], []
    for i in range(DEPTH):
        sh1p, sc1p, g1p, sh2p, sc2p, g2p = jnp.split(adaln(c_ctx[None], w_ada[i], b_ada[i]), 6, axis=-1)
        sh1s, sc1s, g1s, sh2s, sc2s, g2s = jnp.split(adaln(c, w_ada[i], b_ada[i]), 6, axis=-1)
        hp = modulate(rms_norm(xp, norm1_g[i]), sh1p, sc1p)
        hs = modulate(rms_norm(xs, norm1_g[i]), sh1s, sc1s)
        if i % 2 == 0:
            e = i // 2
            lam_init = 0.8 - 0.6 * math.exp(-0.3 * i)
            lam = (jnp.exp(jnp.sum(lam_q1[e].astype(jnp.float32) * lam_k1[e].astype(jnp.float32)))
                   - jnp.exp(jnp.sum(lam_q2[e].astype(jnp.float32) * lam_k2[e].astype(jnp.float32)))
                   + lam_init)
            qa, ka, va, qb, kb, vb = even_projections(hp, w_in_even[e], qn_a[e], kn_a[e], qn_b[e], kn_b[e])
            mp = even_output(diff_attention(qa, ka, va, lam), softmax_attention(qb, kb, vb),
                             subln_g[e], lam_init, w_out_even[e])
            sak.append(ka)
            sav.append(va)
            sbk.append(kb)
            sbv.append(vb)
            qa, ka, va, qb, kb, vb = even_projections(hs, w_in_even[e], qn_a[e], kn_a[e], qn_b[e], kn_b[e])
            qa = axial_rope(qa)
            ka = axial_rope(ka)
            k_all = jnp.concatenate([cache_a_k[:, e], ka], axis=1)
            v_all = jnp.concatenate([cache_a_v[:, e], va], axis=1)
            oa = diff_attention(qa, k_all, v_all, lam)
            ob = neighborhood_attention(qb, kb, vb, cache_b_k[:, e], cache_b_v[:, e], rpb[e])
            ms = even_output(oa, ob, subln_g[e], lam_init, w_out_even[e])
        else:
            j = i // 2
            mp = chunk_mlp(hp, w_in_odd[j], vnorm_g[j], w_sp[j], b_sp[j], w_out_odd[j])
            ms = chunk_mlp(hs, w_in_odd[j], vnorm_g[j], w_sp[j], b_sp[j], w_out_odd[j])
        xp = xp + g1p * mp
        xs = xs + g1s * ms
        hp = modulate(rms_norm(xp, norm2_g[i]), sh2p, sc2p)
        hs = modulate(rms_norm(xs, norm2_g[i]), sh2s, sc2s)
        xp = xp + g2p * conv_ffn(hp, w_up[i], conv_w[i], conv_b[i], w_down[i])
        xs = xs + g2s * conv_ffn(hs, w_up[i], conv_w[i], conv_b[i], w_down[i])
    state_a_k = jnp.stack(sak, axis=1)
    state_a_v = jnp.stack(sav, axis=1)
    state_b_k = jnp.stack(sbk, axis=1)
    state_b_v = jnp.stack(sbv, axis=1)
    return (xp, xs, state_a_k, state_a_v, state_b_k, state_b_v)
```

```python
import functools
import math

import jax
import jax.numpy as jnp
from jax import lax
from jax.experimental import pallas as pl
from jax.experimental.pallas import tpu as pltpu

F32 = jnp.float32
BF16 = jnp.bfloat16

NORM_EPS = 1e-6
ROPE_BASE = 10000.0
GRID_W = 64
HEAD_DIM = 128
N_HEADS_A = 8
N_HEADS_B = 16
SECTION = 2048
NA_ROWS = 8
NA_COLS = 16
CHUNK = 128
N_GROUPS = 8
MASK_VALUE = -1e30
LANE = 128
VMEM_LIMIT = 56 * 1024 * 1024

_NT = (((1,), (1,)), ((), ()))


def _params(semantics):
    return pltpu.CompilerParams(dimension_semantics=semantics,
                                vmem_limit_bytes=VMEM_LIMIT)


def _rms(x):
    return x * lax.rsqrt(jnp.mean(x * x, axis=-1, keepdims=True) + NORM_EPS)


def _adaln_body(c_ref, w_ref, b_ref, o_ref):
    c = c_ref[...]
    s = c * jax.nn.sigmoid(c)
    o_ref[...] = jnp.dot(s.astype(BF16), w_ref[...].astype(BF16),
                         preferred_element_type=F32) + b_ref[...]


def _adaln(cond8, w_ada, b_ada, layer):
    _, d, n = w_ada.shape
    tn = 512
    return pl.pallas_call(
        _adaln_body,
        grid=(n // tn,),
        in_specs=[pl.BlockSpec((8, d), lambda j: (0, 0)),
                  pl.BlockSpec((None, d, tn), lambda j: (layer, 0, j)),
                  pl.BlockSpec((None, 1, tn), lambda j: (layer, 0, j))],
        out_specs=pl.BlockSpec((8, tn), lambda j: (0, j)),
        out_shape=jax.ShapeDtypeStruct((8, n), F32),
        compiler_params=_params(("parallel",)),
        name="adaln",
    )(cond8, w_ada, b_ada.reshape(b_ada.shape[0], 1, n))


class _Group:
    def __init__(self, rows, seq_len, cond_base):
        self.rows = rows
        self.seq_len = seq_len
        self.cond_base = cond_base

    def cond(self, row0):
        if self.cond_base == 0:
            return 0
        return self.cond_base + row0 // self.seq_len


def _mod_spec(group, tm, tn, which, col_of):
    return pl.BlockSpec(
        (None, None, 1, tn),
        lambda *ids: (group.cond(ids[0] * tm), which, 0, col_of(*ids)))


def _norm_mod_body(x_ref, g_ref, sh_ref, sc_ref, o_ref):
    y = _rms(x_ref[...]) * g_ref[...]
    o_ref[...] = (y * (1 + sc_ref[...]) + sh_ref[...]).astype(o_ref.dtype)


def _norm_mod(x, gains, layer, mod, shift_idx, group):
    rows, d = x.shape
    tm = 256
    return pl.pallas_call(
        _norm_mod_body,
        grid=(rows // tm,),
        in_specs=[pl.BlockSpec((tm, d), lambda i: (i, 0)),
                  pl.BlockSpec((None, 1, d), lambda i: (layer, 0, 0)),
                  _mod_spec(group, tm, d, shift_idx, lambda i: 0),
                  _mod_spec(group, tm, d, shift_idx + 1, lambda i: 0)],
        out_specs=pl.BlockSpec((tm, d), lambda i: (i, 0)),
        out_shape=jax.ShapeDtypeStruct((rows, d), BF16),
        compiler_params=_params(("parallel",)),
        name="norm_mod",
    )(x, gains.reshape(gains.shape[0], 1, d), mod, mod)


def _mm_body(*refs, nk, n_extra, epilogue):
    x_ref, w_ref = refs[0], refs[1]
    extras = refs[2:2 + n_extra]
    o_ref = refs[2 + n_extra]
    j = pl.program_id(1)
    part = jnp.dot(x_ref[...], w_ref[...], preferred_element_type=F32)
    if nk == 1:
        epilogue(part, extras, o_ref, j)
        return
    acc_ref = refs[3 + n_extra]
    k = pl.program_id(2)

    @pl.when(k == 0)
    def _():
        acc_ref[...] = part

    @pl.when(k > 0)
    def _():
        acc_ref[...] += part

    @pl.when(k == nk - 1)
    def _():
        epilogue(acc_ref[...], extras, o_ref, j)


def _matmul(x, w, layer, epilogue, extras, extra_specs, out_dtype, *,
            tm, tn, nk=1, name):
    rows, kdim = x.shape
    n = w.shape[-1]
    tk = kdim // nk
    body = functools.partial(_mm_body, nk=nk, n_extra=len(extras),
                             epilogue=epilogue)
    scratch = [pltpu.VMEM((tm, tn), F32)] if nk > 1 else []
    return pl.pallas_call(
        body,
        grid=(rows // tm, n // tn, nk),
        in_specs=[pl.BlockSpec((tm, tk), lambda i, j, k: (i, k)),
                  pl.BlockSpec((None, tk, tn), lambda i, j, k: (layer, k, j)),
                  *extra_specs],
        out_specs=pl.BlockSpec((tm, tn), lambda i, j, k: (i, j)),
        out_shape=jax.ShapeDtypeStruct((rows, n), out_dtype),
        scratch_shapes=scratch,
        compiler_params=_params(("parallel", "parallel", "arbitrary")),
        name=name,
    )(x, w, *extras)


def _plain_epilogue(acc, extras, o_ref, j):
    o_ref[...] = acc.astype(o_ref.dtype)


def _gelu_epilogue(acc, extras, o_ref, j):
    o_ref[...] = (0.5 * acc * (1 + lax.erf(acc * math.sqrt(0.5)))).astype(o_ref.dtype)


def _resid_epilogue(acc, extras, o_ref, j):
    x_ref, gate_ref = extras
    o_ref[...] = x_ref[...] + gate_ref[...] * acc


def _matmul_resid(h, w, layer, x, mod, gate_idx, group, *, tm, tn, nk=1, name):
    specs = [pl.BlockSpec((tm, tn), lambda i, j, k: (i, j)),
             _mod_spec(group, tm, tn, gate_idx, lambda i, j, k: j)]
    return _matmul(h, w, layer, _resid_epilogue, (x, mod), specs, F32,
                   tm=tm, tn=tn, nk=nk, name=name)


def _qkv_epilogue(acc, extras, o_ref, j, *, rope, tn):
    g_ref = extras[0]
    sec = (j * tn) // SECTION
    is_v = jnp.logical_or(sec == 2, sec == 5)

    def normed(c):
        cols = slice(c * LANE, (c + 1) * LANE)
        return _rms(acc[:, cols]) * g_ref[:, cols]

    def store(fn):
        for c in range(tn // LANE):
            o_ref[:, c * LANE:(c + 1) * LANE] = fn(c).astype(o_ref.dtype)

    @pl.when(is_v)
    def _():
        o_ref[...] = acc.astype(o_ref.dtype)

    if not rope:
        @pl.when(jnp.logical_not(is_v))
        def _():
            store(normed)
        return

    cos_ref, sin_lo_ref, sin_hi_ref = extras[1:]
    is_rope = sec < 2

    def rotated(c):
        y = normed(c)
        return (y * cos_ref[...]
                + pltpu.roll(y, LANE - 32, 1) * sin_lo_ref[...]
                + pltpu.roll(y, 32, 1) * sin_hi_ref[...])

    @pl.when(is_rope)
    def _():
        store(rotated)

    @pl.when(jnp.logical_and(jnp.logical_not(is_v), jnp.logical_not(is_rope)))
    def _():
        store(normed)


def _rope_tables(n_tok):
    quarter = HEAD_DIM // 4
    t = jnp.arange(n_tok)
    inv = ROPE_BASE ** (-jnp.arange(quarter, dtype=F32) / quarter)

    def ang(p):
        a = p.astype(F32)[:, None] * inv[None, :]
        return jnp.concatenate([a, a], axis=-1)

    angles = jnp.concatenate([ang(t // GRID_W), ang(t % GRID_W)], axis=-1)
    cos, sin = jnp.cos(angles), jnp.sin(angles)
    low = (jnp.arange(HEAD_DIM) % (2 * quarter)) < quarter
    return cos, jnp.where(low, -sin, 0.0), jnp.where(low, 0.0, sin)


def _qkv_proj(h, w, layer, gains, group, *, rope, out_dtype):
    tm, tn = 1024, 512
    extras = [gains]
    specs = [pl.BlockSpec((1, tn), lambda i, j, k: (0, j))]
    if rope:
        tiles_per_seq = group.seq_len // tm
        tab = pl.BlockSpec((tm, HEAD_DIM), lambda i, j, k: (i % tiles_per_seq, 0))
        extras += list(_rope_tables(group.seq_len))
        specs += [tab, tab, tab]
    epi = functools.partial(_qkv_epilogue, rope=rope, tn=tn)
    return _matmul(h, w, layer, epi, tuple(extras), specs, out_dtype,
                   tm=tm, tn=tn, name="qkv_proj")


def _lam_value(lam_ref, lam_init):
    v = lam_ref[...]
    t1 = jnp.sum(v[0:1] * v[1:2], axis=-1, keepdims=True)
    t2 = jnp.sum(v[2:3] * v[3:4], axis=-1, keepdims=True)
    return jnp.exp(t1) - jnp.exp(t2) + lam_init


def _scores(q, k):
    return lax.dot_general(q.astype(BF16), k.astype(BF16), _NT,
                           preferred_element_type=F32) * (HEAD_DIM ** -0.5)


def _softmax(s):
    e = jnp.exp(s - jnp.max(s, axis=-1, keepdims=True))
    return e / jnp.sum(e, axis=-1, keepdims=True)


def _subln(o, g_ref, lam_init):
    return _rms(o) * g_ref[...] * (1.0 - lam_init)


def _prompt_attn_body(lam_ref, q0, q1, k0, k1, va, qb, kb, vb, g_ref,
                      oa_ref, ob_ref, *, lam_init):
    lam = _lam_value(lam_ref, lam_init)
    a = _softmax(_scores(q0[...], k0[...])) - lam * _softmax(_scores(q1[...], k1[...]))
    o = jnp.dot(a.astype(BF16), va[...].astype(BF16), preferred_element_type=F32)
    oa_ref[...] = _subln(o, g_ref, lam_init).astype(oa_ref.dtype)
    for t in range(2):
        cols = slice(t * HEAD_DIM, (t + 1) * HEAD_DIM)
        p = _softmax(_scores(qb[:, cols], kb[:, cols]))
        ob_ref[:, cols] = jnp.dot(p.astype(BF16), vb[:, cols].astype(BF16),
                                  preferred_element_type=F32).astype(ob_ref.dtype)


def _prompt_attn(qkv, lam_vecs, subln_g, lam_init, seq_len):
    rows = qkv.shape[0]
    nb = rows // seq_len
    sec128 = SECTION // HEAD_DIM
    sec256 = SECTION // (2 * HEAD_DIM)

    def blk(width, col0):
        return pl.BlockSpec((seq_len, width), lambda b, h: (b, col0 + h))

    half = sec128 // 2
    in_specs = [
        pl.BlockSpec((4, HEAD_DIM), lambda b, h: (0, 0)),
        blk(HEAD_DIM, 0), blk(HEAD_DIM, half),
        blk(HEAD_DIM, sec128), blk(HEAD_DIM, sec128 + half),
        blk(2 * HEAD_DIM, 2 * sec256),
        blk(2 * HEAD_DIM, 3 * sec256),
        blk(2 * HEAD_DIM, 4 * sec256),
        blk(2 * HEAD_DIM, 5 * sec256),
        pl.BlockSpec((1, 2 * HEAD_DIM), lambda b, h: (0, 0)),
    ]
    out_spec = pl.BlockSpec((seq_len, 2 * HEAD_DIM), lambda b, h: (b, h))
    out = jax.ShapeDtypeStruct((rows, SECTION), BF16)
    return pl.pallas_call(
        functools.partial(_prompt_attn_body, lam_init=lam_init),
        grid=(nb, N_HEADS_A),
        in_specs=in_specs,
        out_specs=[out_spec, out_spec],
        out_shape=[out, out],
        compiler_params=_params(("parallel", "parallel")),
        name="prompt_attn",
    )(lam_vecs, qkv, qkv, qkv, qkv, qkv, qkv, qkv, qkv, subln_g)


def _latent_diff_body(lam_ref, q0, q1, k0, k1, v, kc0, kc1, vc, g_ref, o_ref,
                      *, lam_init):
    lam = _lam_value(lam_ref, lam_init)

    def probs(q, k, kc):
        s_lat = _scores(q[...], k[...])
        s_ctx = _scores(q[...], kc[...])
        m = jnp.maximum(jnp.max(s_lat, axis=-1, keepdims=True),
                        jnp.max(s_ctx, axis=-1, keepdims=True))
        e_lat = jnp.exp(s_lat - m)
        e_ctx = jnp.exp(s_ctx - m)
        denom = (jnp.sum(e_ctx, axis=-1, keepdims=True)
                 + jnp.sum(e_lat, axis=-1, keepdims=True))
        return e_lat / denom, e_ctx / denom

    p_lat0, p_ctx0 = probs(q0, k0, kc0)
    p_lat1, p_ctx1 = probs(q1, k1, kc1)
    a_lat = p_lat0 - lam * p_lat1
    a_ctx = p_ctx0 - lam * p_ctx1
    o = (jnp.dot(a_ctx.astype(BF16), vc[...].astype(BF16), preferred_element_type=F32)
         + jnp.dot(a_lat.astype(BF16), v[...], preferred_element_type=F32))
    o_ref[...] = _subln(o, g_ref, lam_init).astype(o_ref.dtype)


def _latent_diff_attn(qkv, cache_k, cache_v, lam_vecs, subln_g, lam_init,
                      seq_len, ctx_len):
    rows = qkv.shape[0]
    nb = rows // seq_len
    tq = 256
    nq = seq_len // tq
    sec128 = SECTION // HEAD_DIM
    sec256 = SECTION // (2 * HEAD_DIM)
    half = sec128 // 2

    def qblk(col0):
        return pl.BlockSpec((tq, HEAD_DIM), lambda b, h, i: (b * nq + i, col0 + h))

    def kvblk(n, width, col0):
        return pl.BlockSpec((n, width), lambda b, h, i: (b, col0 + h))

    in_specs = [
        pl.BlockSpec((4, HEAD_DIM), lambda b, h, i: (0, 0)),
        qblk(0), qblk(half),
        kvblk(seq_len, HEAD_DIM, sec128), kvblk(seq_len, HEAD_DIM, sec128 + half),
        kvblk(seq_len, 2 * HEAD_DIM, 2 * sec256),
        kvblk(ctx_len, HEAD_DIM, 0), kvblk(ctx_len, HEAD_DIM, half),
        kvblk(ctx_len, 2 * HEAD_DIM, 0),
        pl.BlockSpec((1, 2 * HEAD_DIM), lambda b, h, i: (0, 0)),
    ]
    return pl.pallas_call(
        functools.partial(_latent_diff_body, lam_init=lam_init),
        grid=(nb, N_HEADS_A, nq),
        in_specs=in_specs,
        out_specs=pl.BlockSpec((tq, 2 * HEAD_DIM), lambda b, h, i: (b * nq + i, h)),
        out_shape=jax.ShapeDtypeStruct((rows, SECTION), BF16),
        compiler_params=_params(("parallel", "parallel", "arbitrary")),
        name="latent_diff_attn",
    )(lam_vecs, qkv, qkv, qkv, qkv, qkv, cache_k, cache_k, cache_v, subln_g)


NA_Q_ROWS = 8
NA_K_ROWS = 16


def _na_band_start(i, grid_rows):
    return jnp.clip(i * NA_Q_ROWS - NA_ROWS // 2, 0, grid_rows - NA_K_ROWS)


def _na_bias_tables(rpb, grid_rows):
    n_blocks = grid_rows // NA_Q_ROWS
    tables = []
    for i in (0, 1, n_blocks - 1):
        r = i * NA_Q_ROWS + jnp.arange(NA_Q_ROWS)
        kr = _na_band_start(i, grid_rows) + jnp.arange(NA_K_ROWS)
        rs = jnp.clip(r - NA_ROWS // 2, 0, grid_rows - NA_ROWS)
        row_ok = (kr[None, :] >= rs[:, None]) & (kr[None, :] < rs[:, None] + NA_ROWS)
        row_idx = jnp.clip(kr[None, :] - r[:, None] + NA_ROWS - 1, 0, 2 * NA_ROWS - 2)
        cols = jnp.arange(GRID_W)
        cstart = jnp.clip(cols - NA_COLS // 2, 0, GRID_W - NA_COLS)
        col_ok = (cols[None, :] >= cstart[:, None]) & (cols[None, :] < cstart[:, None] + NA_COLS)
        col_idx = jnp.clip(cols[None, :] - cols[:, None] + NA_COLS - 1, 0, 2 * NA_COLS - 2)
        bias = rpb[:, row_idx[:, None, :, None], col_idx[None, :, None, :]]
        ok = row_ok[:, None, :, None] & col_ok[None, :, None, :]
        bias = jnp.where(ok[None], bias, MASK_VALUE)
        tables.append(bias.reshape(rpb.shape[0], NA_Q_ROWS * GRID_W, NA_K_ROWS * GRID_W))
    return jnp.stack(tables)


def _latent_na_body(q, k, v, kc, vc, bias, o_ref, *, grid_rows):
    i = pl.program_id(2)
    start = pl.multiple_of(_na_band_start(i, grid_rows) * GRID_W, GRID_W)
    band = pl.ds(start, NA_K_ROWS * GRID_W)
    s_loc = _scores(q[...], k[band, :]) + bias[...]
    s_ctx = _scores(q[...], kc[...])
    m = jnp.maximum(jnp.max(s_loc, axis=-1, keepdims=True),
                    jnp.max(s_ctx, axis=-1, keepdims=True))
    e_loc = jnp.exp(s_loc - m)
    e_ctx = jnp.exp(s_ctx - m)
    denom = (jnp.sum(e_loc, axis=-1, keepdims=True)
             + jnp.sum(e_ctx, axis=-1, keepdims=True))
    p_loc = (e_loc / denom).astype(BF16)
    p_ctx = (e_ctx / denom).astype(BF16)
    o = (jnp.dot(p_loc, v[band, :], preferred_element_type=F32)
         + jnp.dot(p_ctx, vc[...].astype(BF16), preferred_element_type=F32))
    o_ref[...] = o.astype(o_ref.dtype)


def _latent_na_attn(qkv, cache_k, cache_v, rpb, seq_len, ctx_len):
    rows = qkv.shape[0]
    nb = rows // seq_len
    grid_rows = seq_len // GRID_W
    n_blocks = grid_rows // NA_Q_ROWS
    tq = NA_Q_ROWS * GRID_W
    sec128 = SECTION // HEAD_DIM
    bias = _na_bias_tables(rpb, grid_rows)

    def kind(i):
        return jnp.where(i == 0, 0, jnp.where(i == n_blocks - 1, 2, 1))

    def kvblk(n, col0):
        return pl.BlockSpec((n, HEAD_DIM), lambda h, b, i: (b, col0 + h))

    in_specs = [
        pl.BlockSpec((tq, HEAD_DIM), lambda h, b, i: (b * n_blocks + i, 3 * sec128 + h)),
        kvblk(seq_len, 4 * sec128), kvblk(seq_len, 5 * sec128),
        kvblk(ctx_len, 0), kvblk(ctx_len, 0),
        pl.BlockSpec((None, None, tq, NA_K_ROWS * GRID_W),
                     lambda h, b, i: (kind(i), h, 0, 0)),
    ]
    return pl.pallas_call(
        functools.partial(_latent_na_body, grid_rows=grid_rows),
        grid=(N_HEADS_B, nb, n_blocks),
        in_specs=in_specs,
        out_specs=pl.BlockSpec((tq, HEAD_DIM), lambda h, b, i: (b * n_blocks + i, h)),
        out_shape=jax.ShapeDtypeStruct((rows, SECTION), BF16),
        compiler_params=_params(("parallel", "parallel", "arbitrary")),
        name="latent_na_attn",
    )(qkv, qkv, qkv, cache_k, cache_v, bias)


def _spatial_gate_body(u_ref, v_ref, g_ref, ws_ref, bs_ref, o_ref):
    vn = (_rms(v_ref[...]) * g_ref[...]).astype(BF16)
    width = vn.shape[1] // N_GROUPS
    for g in range(N_GROUPS):
        cols = slice(g * width, (g + 1) * width)
        sv = jnp.dot(ws_ref[g].astype(BF16), vn[:, cols],
                     preferred_element_type=F32) + bs_ref[:, g:g + 1]
        o_ref[:, cols] = (u_ref[:, cols] * sv).astype(o_ref.dtype)


def _spatial_gate(z, vnorm_g, w_sp, b_sp, layer):
    rows, two_d = z.shape
    d = two_d // 2
    return pl.pallas_call(
        _spatial_gate_body,
        grid=(rows // CHUNK,),
        in_specs=[pl.BlockSpec((CHUNK, d), lambda n: (n, 0)),
                  pl.BlockSpec((CHUNK, d), lambda n: (n, 1)),
                  pl.BlockSpec((None, 1, d), lambda n: (layer, 0, 0)),
                  pl.BlockSpec((None, N_GROUPS, CHUNK, CHUNK), lambda n: (layer, 0, 0, 0)),
                  pl.BlockSpec((None, CHUNK, N_GROUPS), lambda n: (layer, 0, 0))],
        out_specs=pl.BlockSpec((CHUNK, d), lambda n: (n, 0)),
        out_shape=jax.ShapeDtypeStruct((rows, d), BF16),
        compiler_params=_params(("parallel",)),
        name="spatial_gate",
    )(z, z, vnorm_g.reshape(vnorm_g.shape[0], 1, d), w_sp, jnp.swapaxes(b_sp, 1, 2))


def _conv_gate_body(*refs, tr, seq_len, halo):
    if halo:
        (g_ref, u_ref, g_prev, g_next, u_prev, u_next,
         cwg_ref, cwu_ref, cbg_ref, cbu_ref, o_ref) = refs
    else:
        g_ref, u_ref, cwg_ref, cwu_ref, cbg_ref, cbu_ref, o_ref = refs
        g_prev = g_next = u_prev = u_next = None
    i = pl.program_id(0)
    first = (i * tr) % seq_len == 0
    last = ((i + 1) * tr) % seq_len == 0

    def conv(x_ref, prev_ref, next_ref, cw_ref, cb_ref):
        x = x_ref[...]
        row = lax.broadcasted_iota(jnp.int32, x.shape, 0)
        if halo:
            before = jnp.where(first, 0.0, prev_ref[7:8, :])
            after = jnp.where(last, 0.0, next_ref[0:1, :])
        else:
            before = after = jnp.zeros((1, x.shape[1]), F32)
        x_prev = jnp.where(row == 0, before, pltpu.roll(x, 1, 0))
        x_next = jnp.where(row == tr - 1, after, pltpu.roll(x, tr - 1, 0))
        return (cw_ref[0:1, :] * x_prev + cw_ref[1:2, :] * x
                + cw_ref[2:3, :] * x_next + cb_ref[...])

    g = conv(g_ref, g_prev, g_next, cwg_ref, cbg_ref)
    u = conv(u_ref, u_prev, u_next, cwu_ref, cbu_ref)
    o_ref[...] = (g * jax.nn.sigmoid(g) * u).astype(o_ref.dtype)


def _conv_gate(hu, conv_w, conv_b, layer, group):
    rows, two_f = hu.shape
    d_ff = two_f // 2
    tr = 256
    nj = 2
    tc = d_ff // nj
    halo = group.seq_len > tr
    n8 = rows // 8

    def main(off):
        return pl.BlockSpec((tr, tc), lambda i, j: (i, off + j))

    def prev(off):
        return pl.BlockSpec((8, tc), lambda i, j: (jnp.maximum(i * (tr // 8) - 1, 0), off + j))

    def nxt(off):
        return pl.BlockSpec((8, tc), lambda i, j: (jnp.minimum((i + 1) * (tr // 8), n8 - 1), off + j))

    def cw(off):
        return pl.BlockSpec((None, 3, tc), lambda i, j: (layer, 0, off + j))

    def cb(off):
        return pl.BlockSpec((None, 1, tc), lambda i, j: (layer, 0, off + j))

    in_specs = [main(0), main(nj)]
    args = [hu, hu]
    if halo:
        in_specs += [prev(0), nxt(0), prev(nj), nxt(nj)]
        args += [hu, hu, hu, hu]
    in_specs += [cw(0), cw(nj), cb(0), cb(nj)]
    cb3 = conv_b.reshape(conv_b.shape[0], 1, two_f)
    args += [conv_w, conv_w, cb3, cb3]
    return pl.pallas_call(
        functools.partial(_conv_gate_body, tr=tr, seq_len=group.seq_len, halo=halo),
        grid=(rows // tr, nj),
        in_specs=in_specs,
        out_specs=pl.BlockSpec((tr, tc), lambda i, j: (i, j)),
        out_shape=jax.ShapeDtypeStruct((rows, d_ff), BF16),
        compiler_params=_params(("parallel", "parallel")),
        name="conv_gate",
    )(*args)


def kernel(x_prompt, x_sample, cache_a_k, cache_a_v, cache_b_k, cache_b_v, c, c_ctx,
           w_ada, b_ada, norm1_g, norm2_g,
           w_in_even, qn_a, kn_a, lam_q1, lam_k1, lam_q2, lam_k2, subln_g, qn_b, kn_b, rpb, w_out_even,
           w_in_odd, vnorm_g, w_sp, b_sp, w_out_odd,
           w_up, conv_w, conv_b, w_down):
    batch, seq, d = x_prompt.shape
    dec_batch, dec_seq, _ = x_sample.shape
    past_len = cache_a_k.shape[2]
    depth = w_ada.shape[0]
    n_even = w_in_even.shape[0]

    groups = (_Group(batch * seq, seq, 0), _Group(dec_batch * dec_seq, dec_seq, 1))
    xs = [x_prompt.reshape(batch * seq, d), x_sample.reshape(dec_batch * dec_seq, d)]

    cond8 = jnp.concatenate([c_ctx[None], c, jnp.zeros((8 - 1 - dec_batch, d), F32)])

    w_in_even_b = w_in_even.astype(BF16)
    w_out_even_b = w_out_even.astype(BF16)
    w_in_odd_b = w_in_odd.astype(BF16)
    w_out_odd_b = w_out_odd.astype(BF16)
    w_up_b = w_up.astype(BF16)
    w_down_b = w_down.astype(BF16)

    prompt_qkv = []
    for layer in range(depth):
        mod = _adaln(cond8, w_ada, b_ada, layer).reshape(8, 6, 1, d)
        if layer % 2 == 0:
            e = layer // 2
            lam_init = 0.8 - 0.6 * math.exp(-0.3 * layer)
            lam_vecs = jnp.stack([lam_q1[e], lam_k1[e], lam_q2[e], lam_k2[e]])
            ones = jnp.ones((SECTION,), F32)
            reps = SECTION // HEAD_DIM
            gains = jnp.concatenate([jnp.tile(qn_a[e], reps), jnp.tile(kn_a[e], reps), ones,
                                     jnp.tile(qn_b[e], reps), jnp.tile(kn_b[e], reps), ones])[None]
            sub_g = subln_g[e][None]
            mixed = []
            for gi, group in enumerate(groups):
                h = _norm_mod(xs[gi], norm1_g, layer, mod, 0, group)
                if gi == 0:
                    qkv = _qkv_proj(h, w_in_even_b, e, gains, group, rope=False, out_dtype=F32)
                    prompt_qkv.append(qkv)
                    oa, ob = _prompt_attn(qkv, lam_vecs, sub_g, lam_init, group.seq_len)
                else:
                    qkv = _qkv_proj(h, w_in_even_b, e, gains, group, rope=True, out_dtype=BF16)
                    ck_a = cache_a_k[:, e].reshape(dec_batch * past_len, SECTION)
                    cv_a = cache_a_v[:, e].reshape(dec_batch * past_len, SECTION)
                    ck_b = cache_b_k[:, e].reshape(dec_batch * past_len, SECTION)
                    cv_b = cache_b_v[:, e].reshape(dec_batch * past_len, SECTION)
                    oa = _latent_diff_attn(qkv, ck_a, cv_a, lam_vecs, sub_g, lam_init,
                                           group.seq_len, past_len)
                    ob = _latent_na_attn(qkv, ck_b, cv_b, rpb[e], group.seq_len, past_len)
                mixed.append(jnp.concatenate([oa, ob], axis=-1))
            for gi, group in enumerate(groups):
                xs[gi] = _matmul_resid(mixed[gi], w_out_even_b, e, xs[gi], mod, 2, group,
                                       tm=1024, tn=512, name="even_out_proj")
        else:
            o = layer // 2
            for gi, group in enumerate(groups):
                h = _norm_mod(xs[gi], norm1_g, layer, mod, 0, group)
                z = _matmul(h, w_in_odd_b, o, _gelu_epilogue, (), [], F32,
                            tm=1024, tn=512, name="odd_in_proj")
                gated = _spatial_gate(z, vnorm_g, w_sp, b_sp, o)
                xs[gi] = _matmul_resid(gated, w_out_odd_b, o, xs[gi], mod, 2, group,
                                       tm=1024, tn=512, name="odd_out_proj")
        for gi, group in enumerate(groups):
            h = _norm_mod(xs[gi], norm2_g, layer, mod, 3, group)
            hu = _matmul(h, w_up_b, layer, _plain_epilogue, (), [], F32,
                         tm=1024, tn=512, name="ffn_up")
            act = _conv_gate(hu, conv_w, conv_b, layer, group)
            xs[gi] = _matmul_resid(act, w_down_b, layer, xs[gi], mod, 5, group,
                                   tm=512, tn=512, nk=2, name="ffn_down")

    def state(col0, shape):
        parts = [q[:, col0:col0 + SECTION].reshape((batch, seq) + shape) for q in prompt_qkv]
        return jnp.stack(parts, axis=1)

    h_a, h_b = N_HEADS_A, N_HEADS_B
    return (xs[0].reshape(batch, seq, d),
            xs[1].reshape(dec_batch, dec_seq, d),
            state(1 * SECTION, (2, h_a, HEAD_DIM)),
            state(2 * SECTION, (h_a, 2 * HEAD_DIM)),
            state(4 * SECTION, (h_b, HEAD_DIM)),
            state(5 * SECTION, (h_b, HEAD_DIM)))
```

```python
import functools
import math

import jax
import jax.numpy as jnp
from jax import lax
from jax.experimental import pallas as pl
from jax.experimental.pallas import tpu as pltpu

F32 = jnp.float32
BF16 = jnp.bfloat16

NORM_EPS = 1e-6
ROPE_BASE = 10000.0
GRID_W = 64
HEAD_DIM = 128
N_HEADS_A = 8
N_HEADS_B = 16
SECTION = 2048
NA_ROWS = 8
NA_COLS = 16
CHUNK = 128
N_GROUPS = 8
MASK_VALUE = -1e30
LOG2E = math.log2(math.e)
Q_PRESCALE = HEAD_DIM ** -0.5 * LOG2E
LANE = 128
VMEM_LIMIT = 56 * 1024 * 1024

_NT = (((1,), (1,)), ((), ()))


def _params(semantics):
    return pltpu.CompilerParams(dimension_semantics=semantics,
                                vmem_limit_bytes=VMEM_LIMIT)


def _rms(x):
    return x * lax.rsqrt(jnp.mean(x * x, axis=-1, keepdims=True) + NORM_EPS)


def _adaln_body(c_ref, w_ref, b_ref, o_ref):
    c = c_ref[...]
    s = c * jax.nn.sigmoid(c)
    o_ref[...] = jnp.dot(s.astype(BF16), w_ref[...].astype(BF16),
                         preferred_element_type=F32) + b_ref[...]


def _adaln(cond8, w_ada, b_ada, layer):
    _, d, n = w_ada.shape
    tn = 512
    return pl.pallas_call(
        _adaln_body,
        grid=(n // tn,),
        in_specs=[pl.BlockSpec((8, d), lambda j: (0, 0)),
                  pl.BlockSpec((None, d, tn), lambda j: (layer, 0, j)),
                  pl.BlockSpec((None, 1, tn), lambda j: (layer, 0, j))],
        out_specs=pl.BlockSpec((8, tn), lambda j: (0, j)),
        out_shape=jax.ShapeDtypeStruct((8, n), F32),
        compiler_params=_params(("parallel",)),
        name="adaln",
    )(cond8, w_ada, b_ada.reshape(b_ada.shape[0], 1, n))


class _Group:
    def __init__(self, rows, seq_len, cond_base):
        self.rows = rows
        self.seq_len = seq_len
        self.cond_base = cond_base

    def cond(self, row0):
        if self.cond_base == 0:
            return 0
        return self.cond_base + row0 // self.seq_len


def _mod_spec(group, tm, tn, which, col_of):
    return pl.BlockSpec(
        (None, None, 1, tn),
        lambda *ids: (group.cond(ids[0] * tm), which, 0, col_of(*ids)))


def _norm_mod_body(x_ref, g_ref, sh_ref, sc_ref, o_ref):
    y = _rms(x_ref[...]) * g_ref[...]
    o_ref[...] = (y * (1 + sc_ref[...]) + sh_ref[...]).astype(o_ref.dtype)


def _norm_mod(x, gains, layer, mod, shift_idx, group):
    rows, d = x.shape
    tm = 256
    return pl.pallas_call(
        _norm_mod_body,
        grid=(rows // tm,),
        in_specs=[pl.BlockSpec((tm, d), lambda i: (i, 0)),
                  pl.BlockSpec((None, 1, d), lambda i: (layer, 0, 0)),
                  _mod_spec(group, tm, d, shift_idx, lambda i: 0),
                  _mod_spec(group, tm, d, shift_idx + 1, lambda i: 0)],
        out_specs=pl.BlockSpec((tm, d), lambda i: (i, 0)),
        out_shape=jax.ShapeDtypeStruct((rows, d), BF16),
        compiler_params=_params(("parallel",)),
        name="norm_mod",
    )(x, gains.reshape(gains.shape[0], 1, d), mod, mod)


MM_TM = 1024
MM_TN = 512
MM_CHUNK = 256


def _mm_body(*refs, n_extra, epilogue):
    x_ref, w_ref = refs[0], refs[1]
    extras = refs[2:2 + n_extra]
    o_ref = refs[2 + n_extra]
    j = pl.program_id(1)
    for c in range(x_ref.shape[0] // MM_CHUNK):
        rows = slice(c * MM_CHUNK, (c + 1) * MM_CHUNK)
        acc = jnp.dot(x_ref[rows, :], w_ref[...], preferred_element_type=F32)
        epilogue(acc, extras, o_ref, j, rows)


def _matmul(x, w, layer, epilogue, extras, extra_specs, out_dtype, *,
            k_parts=1, k_part=0, name):
    rows, kdim = x.shape
    n = w.shape[-1]
    tm, tn = MM_TM, MM_TN
    tk = kdim // k_parts
    body = functools.partial(_mm_body, n_extra=len(extras), epilogue=epilogue)
    return pl.pallas_call(
        body,
        grid=(rows // tm, n // tn),
        in_specs=[pl.BlockSpec((tm, tk), lambda i, j: (i, k_part)),
                  pl.BlockSpec((None, tk, tn), lambda i, j: (layer, k_part, j)),
                  *extra_specs],
        out_specs=pl.BlockSpec((tm, tn), lambda i, j: (i, j)),
        out_shape=jax.ShapeDtypeStruct((rows, n), out_dtype),
        compiler_params=_params(("parallel", "arbitrary")),
        name=name,
    )(x, w, *extras)


def _plain_epilogue(acc, extras, o_ref, j, rows):
    o_ref[rows, :] = acc.astype(o_ref.dtype)


def _gelu_epilogue(acc, extras, o_ref, j, rows):
    o_ref[rows, :] = (0.5 * acc * (1 + lax.erf(acc * math.sqrt(0.5)))).astype(o_ref.dtype)


def _resid_epilogue(acc, extras, o_ref, j, rows):
    x_ref, gate_ref = extras[:2]
    if len(extras) == 3:
        acc = extras[2][rows, :] + acc
    o_ref[rows, :] = x_ref[rows, :] + gate_ref[...] * acc


def _matmul_resid(h, w, layer, x, mod, gate_idx, group, *, partial_sum=None,
                  k_parts=1, k_part=0, name):
    tile = pl.BlockSpec((MM_TM, MM_TN), lambda i, j: (i, j))
    specs = [tile, _mod_spec(group, MM_TM, MM_TN, gate_idx, lambda i, j: j)]
    extras = (x, mod)
    if partial_sum is not None:
        specs.append(tile)
        extras += (partial_sum,)
    return _matmul(h, w, layer, _resid_epilogue, extras, specs, F32,
                   k_parts=k_parts, k_part=k_part, name=name)


def _qkv_epilogue(acc, extras, o_ref, j, rows, *, rope):
    g_ref = extras[0]
    sec = (j * MM_TN) // SECTION
    is_v = jnp.logical_or(sec == 2, sec == 5)
    if rope:
        is_rope = sec < 2
        cos = jnp.where(is_rope, extras[1][rows, :], 1.0)
        sin_lo = jnp.where(is_rope, extras[2][rows, :], 0.0)
        sin_hi = jnp.where(is_rope, extras[3][rows, :], 0.0)
    for c in range(MM_TN // LANE):
        cols = slice(c * LANE, (c + 1) * LANE)
        a = acc[:, cols]
        y = jnp.where(is_v, a, _rms(a) * g_ref[:, cols])
        if rope:
            y = (y * cos + pltpu.roll(y, LANE - 32, 1) * sin_lo
                 + pltpu.roll(y, 32, 1) * sin_hi)
        o_ref[rows, cols] = y.astype(o_ref.dtype)


def _rope_tables(n_tok):
    quarter = HEAD_DIM // 4
    t = jnp.arange(n_tok)
    inv = ROPE_BASE ** (-jnp.arange(quarter, dtype=F32) / quarter)

    def ang(p):
        a = p.astype(F32)[:, None] * inv[None, :]
        return jnp.concatenate([a, a], axis=-1)

    angles = jnp.concatenate([ang(t // GRID_W), ang(t % GRID_W)], axis=-1)
    cos, sin = jnp.cos(angles), jnp.sin(angles)
    low = (jnp.arange(HEAD_DIM) % (2 * quarter)) < quarter
    return cos, jnp.where(low, -sin, 0.0), jnp.where(low, 0.0, sin)


def _qkv_proj(h, w, layer, gains, group, *, rope, out_dtype):
    extras = [gains]
    specs = [pl.BlockSpec((1, MM_TN), lambda i, j: (0, j))]
    if rope:
        tiles_per_seq = group.seq_len // MM_TM
        tab = pl.BlockSpec((MM_TM, HEAD_DIM), lambda i, j: (i % tiles_per_seq, 0))
        extras += list(_rope_tables(group.seq_len))
        specs += [tab, tab, tab]
    epi = functools.partial(_qkv_epilogue, rope=rope)
    return _matmul(h, w, layer, epi, tuple(extras), specs, out_dtype, name="qkv_proj")


def _lam_value(lam_ref, lam_init):
    v = lam_ref[...]
    t1 = jnp.sum(v[0:1] * v[1:2], axis=-1, keepdims=True)
    t2 = jnp.sum(v[2:3] * v[3:4], axis=-1, keepdims=True)
    return jnp.exp(t1) - jnp.exp(t2) + lam_init


def _scores(q, k):
    return lax.dot_general(q.astype(BF16), k.astype(BF16), _NT,
                           preferred_element_type=F32)


def _softmax(s):
    e = jnp.exp2(s - jnp.max(s, axis=-1, keepdims=True))
    return e / jnp.sum(e, axis=-1, keepdims=True)


def _joint_exp(s_a, s_b):
    m = jnp.maximum(jnp.max(s_a, axis=-1, keepdims=True),
                    jnp.max(s_b, axis=-1, keepdims=True))
    e_a = jnp.exp2(s_a - m)
    e_b = jnp.exp2(s_b - m)
    denom = (jnp.sum(e_a, axis=-1, keepdims=True)
             + jnp.sum(e_b, axis=-1, keepdims=True))
    return e_a.astype(BF16), e_b.astype(BF16), denom


def _subln(o, g_ref, lam_init):
    return _rms(o) * g_ref[...] * (1.0 - lam_init)


def _prompt_attn_body(lam_ref, q0, q1, k0, k1, va, qb, kb, vb, g_ref,
                      oa_ref, ob_ref, *, lam_init):
    lam = _lam_value(lam_ref, lam_init)
    a = _softmax(_scores(q0[...], k0[...])) - lam * _softmax(_scores(q1[...], k1[...]))
    o = jnp.dot(a.astype(BF16), va[...].astype(BF16), preferred_element_type=F32)
    oa_ref[...] = _subln(o, g_ref, lam_init).astype(oa_ref.dtype)
    for t in range(2):
        cols = slice(t * HEAD_DIM, (t + 1) * HEAD_DIM)
        p = _softmax(_scores(qb[:, cols], kb[:, cols]))
        ob_ref[:, cols] = jnp.dot(p.astype(BF16), vb[:, cols].astype(BF16),
                                  preferred_element_type=F32).astype(ob_ref.dtype)


def _prompt_attn(qkv, lam_vecs, subln_g, lam_init, seq_len):
    rows = qkv.shape[0]
    nb = rows // seq_len
    sec128 = SECTION // HEAD_DIM
    sec256 = SECTION // (2 * HEAD_DIM)

    def blk(width, col0):
        return pl.BlockSpec((seq_len, width), lambda b, h: (b, col0 + h))

    half = sec128 // 2
    in_specs = [
        pl.BlockSpec((4, HEAD_DIM), lambda b, h: (0, 0)),
        blk(HEAD_DIM, 0), blk(HEAD_DIM, half),
        blk(HEAD_DIM, sec128), blk(HEAD_DIM, sec128 + half),
        blk(2 * HEAD_DIM, 2 * sec256),
        blk(2 * HEAD_DIM, 3 * sec256),
        blk(2 * HEAD_DIM, 4 * sec256),
        blk(2 * HEAD_DIM, 5 * sec256),
        pl.BlockSpec((1, 2 * HEAD_DIM), lambda b, h: (0, 0)),
    ]
    out_spec = pl.BlockSpec((seq_len, 2 * HEAD_DIM), lambda b, h: (b, h))
    out = jax.ShapeDtypeStruct((rows, SECTION), BF16)
    return pl.pallas_call(
        functools.partial(_prompt_attn_body, lam_init=lam_init),
        grid=(nb, N_HEADS_A),
        in_specs=in_specs,
        out_specs=[out_spec, out_spec],
        out_shape=[out, out],
        compiler_params=_params(("parallel", "parallel")),
        name="prompt_attn",
    )(lam_vecs, qkv, qkv, qkv, qkv, qkv, qkv, qkv, qkv, subln_g)


def _latent_diff_body(lam_ref, q0, q1, k0, k1, v, kc0, kc1, vc, g_ref, o_ref,
                      *, lam_init):
    lam = _lam_value(lam_ref, lam_init)
    v_ctx = vc[...].astype(BF16)

    def attend(q, k, kc):
        e_lat, e_ctx, denom = _joint_exp(_scores(q[...], k[...]), _scores(q[...], kc[...]))
        o = (jnp.dot(e_ctx, v_ctx, preferred_element_type=F32)
             + jnp.dot(e_lat, v[...], preferred_element_type=F32))
        return o / denom

    o = attend(q0, k0, kc0) - lam * attend(q1, k1, kc1)
    o_ref[...] = _subln(o, g_ref, lam_init).astype(o_ref.dtype)


def _latent_diff_attn(qkv, cache_k, cache_v, lam_vecs, subln_g, lam_init,
                      seq_len, ctx_len):
    rows = qkv.shape[0]
    nb = rows // seq_len
    tq = 256
    nq = seq_len // tq
    sec128 = SECTION // HEAD_DIM
    sec256 = SECTION // (2 * HEAD_DIM)
    half = sec128 // 2

    def qblk(col0):
        return pl.BlockSpec((tq, HEAD_DIM), lambda b, h, i: (b * nq + i, col0 + h))

    def kvblk(n, width, col0):
        return pl.BlockSpec((n, width), lambda b, h, i: (b, col0 + h))

    in_specs = [
        pl.BlockSpec((4, HEAD_DIM), lambda b, h, i: (0, 0)),
        qblk(0), qblk(half),
        kvblk(seq_len, HEAD_DIM, sec128), kvblk(seq_len, HEAD_DIM, sec128 + half),
        kvblk(seq_len, 2 * HEAD_DIM, 2 * sec256),
        kvblk(ctx_len, HEAD_DIM, 0), kvblk(ctx_len, HEAD_DIM, half),
        kvblk(ctx_len, 2 * HEAD_DIM, 0),
        pl.BlockSpec((1, 2 * HEAD_DIM), lambda b, h, i: (0, 0)),
    ]
    return pl.pallas_call(
        functools.partial(_latent_diff_body, lam_init=lam_init),
        grid=(nb, N_HEADS_A, nq),
        in_specs=in_specs,
        out_specs=pl.BlockSpec((tq, 2 * HEAD_DIM), lambda b, h, i: (b * nq + i, h)),
        out_shape=jax.ShapeDtypeStruct((rows, SECTION), BF16),
        compiler_params=_params(("parallel", "parallel", "arbitrary")),
        name="latent_diff_attn",
    )(lam_vecs, qkv, qkv, qkv, qkv, qkv, cache_k, cache_k, cache_v, subln_g)


NA_Q_ROWS = 8
NA_K_ROWS = 16


def _na_band_start(i, grid_rows):
    return jnp.clip(i * NA_Q_ROWS - NA_ROWS // 2, 0, grid_rows - NA_K_ROWS)


NA_D_MIN = -NA_Q_ROWS
NA_D_MAX = NA_K_ROWS + NA_ROWS - 2


def _na_pair_tables(rpb):
    n_h, n_dr, n_dc = rpb.shape
    u = jnp.concatenate([rpb[..., NA_COLS - 1:],
                         jnp.zeros((n_h, n_dr, LANE - n_dc), F32),
                         rpb[..., :NA_COLS - 1]], axis=-1)
    skew = jnp.tile(u, (1, 1, GRID_W))[..., :GRID_W * (LANE - 1)]
    toep = skew.reshape(n_h, n_dr, GRID_W, LANE - 1)[..., :GRID_W]
    cols = jnp.arange(GRID_W)
    cstart = jnp.clip(cols - NA_COLS // 2, 0, GRID_W - NA_COLS)
    col_ok = (cols[None, :] >= cstart[:, None]) & (cols[None, :] < cstart[:, None] + NA_COLS)
    toep = jnp.where(col_ok, toep, MASK_VALUE)
    ext = jnp.pad(toep, ((0, 0), (-NA_D_MIN, NA_D_MAX + 1 - n_dr), (0, 0), (0, 0)),
                  constant_values=MASK_VALUE)
    return jnp.concatenate([ext[:, :-1], ext[:, 1:]], axis=-1)


def _latent_na_body(q, k, v, kc, vc, tab_ref, o_ref, bias_ref, *, grid_rows):
    i = pl.program_id(2)
    n_blocks = grid_rows // NA_Q_ROWS
    row0 = i * NA_Q_ROWS
    band0 = _na_band_start(i, grid_rows)

    @pl.when(jnp.logical_or(i <= 1, i == n_blocks - 1))
    def _():
        lane = lax.broadcasted_iota(jnp.int32, (GRID_W, LANE), 1)
        for rq in range(NA_Q_ROWS):
            r = row0 + rq
            rs = jnp.clip(r - NA_ROWS // 2, 0, grid_rows - NA_ROWS)

            def penalty(kr):
                ok = jnp.logical_and(kr >= rs, kr < rs + NA_ROWS)
                return jnp.where(ok, 0.0, MASK_VALUE)

            for p in range(NA_K_ROWS // 2):
                kr = band0 + 2 * p
                d = kr - r + NA_ROWS - 1
                pen = jnp.where(lane < GRID_W, penalty(kr), penalty(kr + 1))
                bias_ref[rq * GRID_W:(rq + 1) * GRID_W, p * LANE:(p + 1) * LANE] = (
                    tab_ref[d - NA_D_MIN] * LOG2E + pen)

    start = pl.multiple_of(band0 * GRID_W, GRID_W)
    band = pl.ds(start, NA_K_ROWS * GRID_W)
    e_loc, e_ctx, denom = _joint_exp(_scores(q[...], k[band, :]) + bias_ref[...],
                                     _scores(q[...], kc[...]))
    o = (jnp.dot(e_loc, v[band, :], preferred_element_type=F32)
         + jnp.dot(e_ctx, vc[...].astype(BF16), preferred_element_type=F32))
    o_ref[...] = (o / denom).astype(o_ref.dtype)


def _latent_na_attn(qkv, cache_k, cache_v, rpb, seq_len, ctx_len):
    rows = qkv.shape[0]
    nb = rows // seq_len
    grid_rows = seq_len // GRID_W
    n_blocks = grid_rows // NA_Q_ROWS
    tq = NA_Q_ROWS * GRID_W
    sec128 = SECTION // HEAD_DIM
    tables = _na_pair_tables(rpb)

    def kvblk(n, col0):
        return pl.BlockSpec((n, HEAD_DIM), lambda h, b, i: (b, col0 + h))

    in_specs = [
        pl.BlockSpec((tq, HEAD_DIM), lambda h, b, i: (b * n_blocks + i, 3 * sec128 + h)),
        kvblk(seq_len, 4 * sec128), kvblk(seq_len, 5 * sec128),
        kvblk(ctx_len, 0), kvblk(ctx_len, 0),
        pl.BlockSpec((None,) + tables.shape[1:], lambda h, b, i: (h, 0, 0, 0)),
    ]
    return pl.pallas_call(
        functools.partial(_latent_na_body, grid_rows=grid_rows),
        grid=(N_HEADS_B, nb, n_blocks),
        in_specs=in_specs,
        out_specs=pl.BlockSpec((tq, HEAD_DIM), lambda h, b, i: (b * n_blocks + i, h)),
        out_shape=jax.ShapeDtypeStruct((rows, SECTION), BF16),
        scratch_shapes=[pltpu.VMEM((tq, NA_K_ROWS * GRID_W), F32)],
        compiler_params=_params(("arbitrary", "arbitrary", "arbitrary")),
        name="latent_na_attn",
    )(qkv, qkv, qkv, cache_k, cache_v, tables)


def _spatial_gate_body(u_ref, v_ref, g_ref, ws_ref, bs_ref, o_ref):
    vn = (_rms(v_ref[...]) * g_ref[...]).astype(BF16)
    width = vn.shape[1] // N_GROUPS
    for g in range(N_GROUPS):
        cols = slice(g * width, (g + 1) * width)
        sv = jnp.dot(ws_ref[g].astype(BF16), vn[:, cols],
                     preferred_element_type=F32) + bs_ref[:, g:g + 1]
        o_ref[:, cols] = (u_ref[:, cols] * sv).astype(o_ref.dtype)


def _spatial_gate(z, vnorm_g, w_sp, b_sp, layer):
    rows, two_d = z.shape
    d = two_d // 2
    return pl.pallas_call(
        _spatial_gate_body,
        grid=(rows // CHUNK,),
        in_specs=[pl.BlockSpec((CHUNK, d), lambda n: (n, 0)),
                  pl.BlockSpec((CHUNK, d), lambda n: (n, 1)),
                  pl.BlockSpec((None, 1, d), lambda n: (layer, 0, 0)),
                  pl.BlockSpec((None, N_GROUPS, CHUNK, CHUNK), lambda n: (layer, 0, 0, 0)),
                  pl.BlockSpec((None, CHUNK, N_GROUPS), lambda n: (layer, 0, 0))],
        out_specs=pl.BlockSpec((CHUNK, d), lambda n: (n, 0)),
        out_shape=jax.ShapeDtypeStruct((rows, d), BF16),
        compiler_params=_params(("parallel",)),
        name="spatial_gate",
    )(z, z, vnorm_g.reshape(vnorm_g.shape[0], 1, d), w_sp, jnp.swapaxes(b_sp, 1, 2))


def _conv_gate_body(*refs, tr, seq_len, halo):
    if halo:
        (g_ref, u_ref, g_prev, g_next, u_prev, u_next,
         cwg_ref, cwu_ref, cbg_ref, cbu_ref, o_ref) = refs
    else:
        g_ref, u_ref, cwg_ref, cwu_ref, cbg_ref, cbu_ref, o_ref = refs
        g_prev = g_next = u_prev = u_next = None
    i = pl.program_id(0)
    first = (i * tr) % seq_len == 0
    last = ((i + 1) * tr) % seq_len == 0

    def conv(x_ref, prev_ref, next_ref, cw_ref, cb_ref):
        x = x_ref[...]
        row = lax.broadcasted_iota(jnp.int32, x.shape, 0)
        if halo:
            before = jnp.where(first, 0.0, prev_ref[7:8, :])
            after = jnp.where(last, 0.0, next_ref[0:1, :])
        else:
            before = after = jnp.zeros((1, x.shape[1]), F32)
        x_prev = jnp.where(row == 0, before, pltpu.roll(x, 1, 0))
        x_next = jnp.where(row == tr - 1, after, pltpu.roll(x, tr - 1, 0))
        return (cw_ref[0:1, :] * x_prev + cw_ref[1:2, :] * x
                + cw_ref[2:3, :] * x_next + cb_ref[...])

    g = conv(g_ref, g_prev, g_next, cwg_ref, cbg_ref)
    u = conv(u_ref, u_prev, u_next, cwu_ref, cbu_ref)
    o_ref[...] = (g * jax.nn.sigmoid(g) * u).astype(o_ref.dtype)


def _conv_gate(hu, conv_w, conv_b, layer, group):
    rows, two_f = hu.shape
    d_ff = two_f // 2
    tr = 256
    nj = 2
    tc = d_ff // nj
    halo = group.seq_len > tr
    n8 = rows // 8

    def main(off):
        return pl.BlockSpec((tr, tc), lambda i, j: (i, off + j))

    def prev(off):
        return pl.BlockSpec((8, tc), lambda i, j: (jnp.maximum(i * (tr // 8) - 1, 0), off + j))

    def nxt(off):
        return pl.BlockSpec((8, tc), lambda i, j: (jnp.minimum((i + 1) * (tr // 8), n8 - 1), off + j))

    def cw(off):
        return pl.BlockSpec((None, 3, tc), lambda i, j: (layer, 0, off + j))

    def cb(off):
        return pl.BlockSpec((None, 1, tc), lambda i, j: (layer, 0, off + j))

    in_specs = [main(0), main(nj)]
    args = [hu, hu]
    if halo:
        in_specs += [prev(0), nxt(0), prev(nj), nxt(nj)]
        args += [hu, hu, hu, hu]
    in_specs += [cw(0), cw(nj), cb(0), cb(nj)]
    cb3 = conv_b.reshape(conv_b.shape[0], 1, two_f)
    args += [conv_w, conv_w, cb3, cb3]
    return pl.pallas_call(
        functools.partial(_conv_gate_body, tr=tr, seq_len=group.seq_len, halo=halo),
        grid=(rows // tr, nj),
        in_specs=in_specs,
        out_specs=pl.BlockSpec((tr, tc), lambda i, j: (i, j)),
        out_shape=jax.ShapeDtypeStruct((rows, d_ff), BF16),
        compiler_params=_params(("parallel", "parallel")),
        name="conv_gate",
    )(*args)


def kernel(x_prompt, x_sample, cache_a_k, cache_a_v, cache_b_k, cache_b_v, c, c_ctx,
           w_ada, b_ada, norm1_g, norm2_g,
           w_in_even, qn_a, kn_a, lam_q1, lam_k1, lam_q2, lam_k2, subln_g, qn_b, kn_b, rpb, w_out_even,
           w_in_odd, vnorm_g, w_sp, b_sp, w_out_odd,
           w_up, conv_w, conv_b, w_down):
    batch, seq, d = x_prompt.shape
    dec_batch, dec_seq, _ = x_sample.shape
    past_len = cache_a_k.shape[2]
    depth = w_ada.shape[0]
    n_even = w_in_even.shape[0]

    groups = (_Group(batch * seq, seq, 0), _Group(dec_batch * dec_seq, dec_seq, 1))
    xs = [x_prompt.reshape(batch * seq, d), x_sample.reshape(dec_batch * dec_seq, d)]

    cond8 = jnp.concatenate([c_ctx[None], c, jnp.zeros((8 - 1 - dec_batch, d), F32)])

    w_in_even_b = w_in_even.astype(BF16)
    w_out_even_b = w_out_even.astype(BF16)
    w_in_odd_b = w_in_odd.astype(BF16)
    w_out_odd_b = w_out_odd.astype(BF16)
    w_up_b = w_up.astype(BF16)
    w_down_b = w_down.astype(BF16)

    prompt_qkv = []
    for layer in range(depth):
        mod = _adaln(cond8, w_ada, b_ada, layer).reshape(8, 6, 1, d)
        if layer % 2 == 0:
            e = layer // 2
            lam_init = 0.8 - 0.6 * math.exp(-0.3 * layer)
            lam_vecs = jnp.stack([lam_q1[e], lam_k1[e], lam_q2[e], lam_k2[e]])
            ones = jnp.ones((SECTION,), F32)
            reps = SECTION // HEAD_DIM
            gains = jnp.concatenate([jnp.tile(qn_a[e] * Q_PRESCALE, reps), jnp.tile(kn_a[e], reps), ones,
                                     jnp.tile(qn_b[e] * Q_PRESCALE, reps), jnp.tile(kn_b[e], reps), ones])[None]
            sub_g = subln_g[e][None]
            mixed = []
            for gi, group in enumerate(groups):
                h = _norm_mod(xs[gi], norm1_g, layer, mod, 0, group)
                if gi == 0:
                    qkv = _qkv_proj(h, w_in_even_b, e, gains, group, rope=False, out_dtype=F32)
                    prompt_qkv.append(qkv)
                    oa, ob = _prompt_attn(qkv, lam_vecs, sub_g, lam_init, group.seq_len)
                else:
                    qkv = _qkv_proj(h, w_in_even_b, e, gains, group, rope=True, out_dtype=BF16)
                    ck_a = cache_a_k[:, e].reshape(dec_batch * past_len, SECTION)
                    cv_a = cache_a_v[:, e].reshape(dec_batch * past_len, SECTION)
                    ck_b = cache_b_k[:, e].reshape(dec_batch * past_len, SECTION)
                    cv_b = cache_b_v[:, e].reshape(dec_batch * past_len, SECTION)
                    oa = _latent_diff_attn(qkv, ck_a, cv_a, lam_vecs, sub_g, lam_init,
                                           group.seq_len, past_len)
                    ob = _latent_na_attn(qkv, ck_b, cv_b, rpb[e], group.seq_len, past_len)
                mixed.append(jnp.concatenate([oa, ob], axis=-1))
            for gi, group in enumerate(groups):
                xs[gi] = _matmul_resid(mixed[gi], w_out_even_b, e, xs[gi], mod, 2, group,
                                       name="even_out_proj")
        else:
            o = layer // 2
            for gi, group in enumerate(groups):
                h = _norm_mod(xs[gi], norm1_g, layer, mod, 0, group)
                z = _matmul(h, w_in_odd_b, o, _gelu_epilogue, (), [], F32, name="odd_in_proj")
                gated = _spatial_gate(z, vnorm_g, w_sp, b_sp, o)
                xs[gi] = _matmul_resid(gated, w_out_odd_b, o, xs[gi], mod, 2, group,
                                       name="odd_out_proj")
        for gi, group in enumerate(groups):
            h = _norm_mod(xs[gi], norm2_g, layer, mod, 3, group)
            hu = _matmul(h, w_up_b, layer, _plain_epilogue, (), [], F32, name="ffn_up")
            act = _conv_gate(hu, conv_w, conv_b, layer, group)
            half = _matmul(act, w_down_b, layer, _plain_epilogue, (), [], F32,
                           k_parts=2, k_part=0, name="ffn_down_lo")
            xs[gi] = _matmul_resid(act, w_down_b, layer, xs[gi], mod, 5, group, partial_sum=half,
                                   k_parts=2, k_part=1, name="ffn_down_hi")

    def state(col0, shape):
        parts = [q[:, col0:col0 + SECTION].reshape((batch, seq) + shape) for q in prompt_qkv]
        return jnp.stack(parts, axis=1)

    h_a, h_b = N_HEADS_A, N_HEADS_B
    return (xs[0].reshape(batch, seq, d),
            xs[1].reshape(dec_batch, dec_seq, d),
            state(1 * SECTION, (2, h_a, HEAD_DIM)),
            state(2 * SECTION, (h_a, 2 * HEAD_DIM)),
            state(4 * SECTION, (h_b, HEAD_DIM)),
            state(5 * SECTION, (h_b, HEAD_DIM)))
```

```python
import functools
import math

import jax
import jax.numpy as jnp
from jax import lax
from jax.experimental import pallas as pl
from jax.experimental.pallas import tpu as pltpu

F32 = jnp.float32
BF16 = jnp.bfloat16

NORM_EPS = 1e-6
ROPE_BASE = 10000.0
GRID_W = 64
HEAD_DIM = 128
N_HEADS_A = 8
N_HEADS_B = 16
SECTION = 2048
NA_ROWS = 8
NA_COLS = 16
CHUNK = 128
N_GROUPS = 8
MASK_VALUE = -1e30
LOG2E = math.log2(math.e)
Q_PRESCALE = HEAD_DIM ** -0.5 * LOG2E
LANE = 128
VMEM_LIMIT = 56 * 1024 * 1024

_NT = (((1,), (1,)), ((), ()))


def _params(semantics):
    return pltpu.CompilerParams(dimension_semantics=semantics,
                                vmem_limit_bytes=VMEM_LIMIT)


def _rms(x):
    return x * lax.rsqrt(jnp.mean(x * x, axis=-1, keepdims=True) + NORM_EPS)


def _adaln_body(c_ref, w_ref, b_ref, o_ref):
    c = c_ref[...]
    s = c * jax.nn.sigmoid(c)
    o_ref[...] = jnp.dot(s.astype(BF16), w_ref[...].astype(BF16),
                         preferred_element_type=F32) + b_ref[...]


def _adaln(cond8, w_ada, b_ada, layer):
    _, d, n = w_ada.shape
    tn = 512
    return pl.pallas_call(
        _adaln_body,
        grid=(n // tn,),
        in_specs=[pl.BlockSpec((8, d), lambda j: (0, 0)),
                  pl.BlockSpec((None, d, tn), lambda j: (layer, 0, j)),
                  pl.BlockSpec((None, 1, tn), lambda j: (layer, 0, j))],
        out_specs=pl.BlockSpec((8, tn), lambda j: (0, j)),
        out_shape=jax.ShapeDtypeStruct((8, n), F32),
        compiler_params=_params(("parallel",)),
        name="adaln",
    )(cond8, w_ada, b_ada.reshape(b_ada.shape[0], 1, n))


class _Group:
    def __init__(self, rows, seq_len, cond_base):
        self.rows = rows
        self.seq_len = seq_len
        self.cond_base = cond_base

    def cond(self, row0):
        if self.cond_base == 0:
            return 0
        return self.cond_base + row0 // self.seq_len


def _mod_spec(group, tm, tn, which, col_of):
    return pl.BlockSpec(
        (None, None, 1, tn),
        lambda *ids: (group.cond(ids[0] * tm), which, 0, col_of(*ids)))


def _norm_mod_body(x_ref, g_ref, sh_ref, sc_ref, o_ref):
    y = _rms(x_ref[...]) * g_ref[...]
    o_ref[...] = (y * (1 + sc_ref[...]) + sh_ref[...]).astype(o_ref.dtype)


def _norm_mod(x, gains, layer, mod, shift_idx, group):
    rows, d = x.shape
    tm = 256
    return pl.pallas_call(
        _norm_mod_body,
        grid=(rows // tm,),
        in_specs=[pl.BlockSpec((tm, d), lambda i: (i, 0)),
                  pl.BlockSpec((None, 1, d), lambda i: (layer, 0, 0)),
                  _mod_spec(group, tm, d, shift_idx, lambda i: 0),
                  _mod_spec(group, tm, d, shift_idx + 1, lambda i: 0)],
        out_specs=pl.BlockSpec((tm, d), lambda i: (i, 0)),
        out_shape=jax.ShapeDtypeStruct((rows, d), BF16),
        compiler_params=_params(("parallel",)),
        name="norm_mod",
    )(x, gains.reshape(gains.shape[0], 1, d), mod, mod)


MM_TM = 1024
MM_TN = 512
MM_CHUNK = 256


def _mm_body(*refs, n_extra, epilogue):
    x_ref, w_ref = refs[0], refs[1]
    extras = refs[2:2 + n_extra]
    o_ref = refs[2 + n_extra]
    j = pl.program_id(1)
    for c in range(x_ref.shape[0] // MM_CHUNK):
        rows = slice(c * MM_CHUNK, (c + 1) * MM_CHUNK)
        acc = jnp.dot(x_ref[rows, :], w_ref[...], preferred_element_type=F32)
        epilogue(acc, extras, o_ref, j, rows)


def _matmul(x, w, layer, epilogue, extras, extra_specs, out_dtype, *,
            k_parts=1, k_part=0, name):
    rows, kdim = x.shape
    n = w.shape[-1]
    tm, tn = MM_TM, MM_TN
    tk = kdim // k_parts
    body = functools.partial(_mm_body, n_extra=len(extras), epilogue=epilogue)
    return pl.pallas_call(
        body,
        grid=(rows // tm, n // tn),
        in_specs=[pl.BlockSpec((tm, tk), lambda i, j: (i, k_part)),
                  pl.BlockSpec((None, tk, tn), lambda i, j: (layer, k_part, j)),
                  *extra_specs],
        out_specs=pl.BlockSpec((tm, tn), lambda i, j: (i, j)),
        out_shape=jax.ShapeDtypeStruct((rows, n), out_dtype),
        compiler_params=_params(("parallel", "arbitrary")),
        name=name,
    )(x, w, *extras)


def _plain_epilogue(acc, extras, o_ref, j, rows):
    o_ref[rows, :] = acc.astype(o_ref.dtype)


def _gelu_epilogue(acc, extras, o_ref, j, rows):
    o_ref[rows, :] = (0.5 * acc * (1 + lax.erf(acc * math.sqrt(0.5)))).astype(o_ref.dtype)


def _resid_epilogue(acc, extras, o_ref, j, rows):
    x_ref, gate_ref = extras[:2]
    if len(extras) == 3:
        acc = extras[2][rows, :] + acc
    o_ref[rows, :] = x_ref[rows, :] + gate_ref[...] * acc


def _matmul_resid(h, w, layer, x, mod, gate_idx, group, *, partial_sum=None,
                  k_parts=1, k_part=0, name):
    tile = pl.BlockSpec((MM_TM, MM_TN), lambda i, j: (i, j))
    specs = [tile, _mod_spec(group, MM_TM, MM_TN, gate_idx, lambda i, j: j)]
    extras = (x, mod)
    if partial_sum is not None:
        specs.append(tile)
        extras += (partial_sum,)
    return _matmul(h, w, layer, _resid_epilogue, extras, specs, F32,
                   k_parts=k_parts, k_part=k_part, name=name)


def _qkv_epilogue(acc, extras, o_ref, j, rows, *, rope):
    g_ref = extras[0]
    sec = (j * MM_TN) // SECTION
    is_v = jnp.logical_or(sec == 2, sec == 5)
    if rope:
        is_rope = sec < 2
        cos = jnp.where(is_rope, extras[1][rows, :], 1.0)
        sin_lo = jnp.where(is_rope, extras[2][rows, :], 0.0)
        sin_hi = jnp.where(is_rope, extras[3][rows, :], 0.0)
    for c in range(MM_TN // LANE):
        cols = slice(c * LANE, (c + 1) * LANE)
        a = acc[:, cols]
        y = jnp.where(is_v, a, _rms(a) * g_ref[:, cols])
        if rope:
            y = (y * cos + pltpu.roll(y, LANE - 32, 1) * sin_lo
                 + pltpu.roll(y, 32, 1) * sin_hi)
        o_ref[rows, cols] = y.astype(o_ref.dtype)


def _rope_tables(n_tok):
    quarter = HEAD_DIM // 4
    t = jnp.arange(n_tok)
    inv = ROPE_BASE ** (-jnp.arange(quarter, dtype=F32) / quarter)

    def ang(p):
        a = p.astype(F32)[:, None] * inv[None, :]
        return jnp.concatenate([a, a], axis=-1)

    angles = jnp.concatenate([ang(t // GRID_W), ang(t % GRID_W)], axis=-1)
    cos, sin = jnp.cos(angles), jnp.sin(angles)
    low = (jnp.arange(HEAD_DIM) % (2 * quarter)) < quarter
    return cos, jnp.where(low, -sin, 0.0), jnp.where(low, 0.0, sin)


def _qkv_proj(h, w, layer, gains, group, *, rope, out_dtype):
    extras = [gains]
    specs = [pl.BlockSpec((1, MM_TN), lambda i, j: (0, j))]
    if rope:
        tiles_per_seq = group.seq_len // MM_TM
        tab = pl.BlockSpec((MM_TM, HEAD_DIM), lambda i, j: (i % tiles_per_seq, 0))
        extras += list(_rope_tables(group.seq_len))
        specs += [tab, tab, tab]
    epi = functools.partial(_qkv_epilogue, rope=rope)
    return _matmul(h, w, layer, epi, tuple(extras), specs, out_dtype, name="qkv_proj")


def _lam_value(lam_ref, lam_init):
    v = lam_ref[...]
    t1 = jnp.sum(v[0:1] * v[1:2], axis=-1, keepdims=True)
    t2 = jnp.sum(v[2:3] * v[3:4], axis=-1, keepdims=True)
    return jnp.exp(t1) - jnp.exp(t2) + lam_init


def _scores(q, k):
    return lax.dot_general(q.astype(BF16), k.astype(BF16), _NT,
                           preferred_element_type=F32)


def _softmax(s):
    e = jnp.exp2(s - jnp.max(s, axis=-1, keepdims=True))
    return e / jnp.sum(e, axis=-1, keepdims=True)


def _joint_exp(s_a, s_b):
    m = jnp.maximum(jnp.max(s_a, axis=-1, keepdims=True),
                    jnp.max(s_b, axis=-1, keepdims=True))
    e_a = jnp.exp2(s_a - m)
    e_b = jnp.exp2(s_b - m)
    denom = (jnp.sum(e_a, axis=-1, keepdims=True)
             + jnp.sum(e_b, axis=-1, keepdims=True))
    return e_a.astype(BF16), e_b.astype(BF16), denom


def _subln(o, g_ref, lam_init):
    return _rms(o) * g_ref[...] * (1.0 - lam_init)


def _prompt_attn_body(lam_ref, q0, q1, k0, k1, va, qb, kb, vb, g_ref,
                      oa_ref, ob_ref, *, lam_init):
    lam = _lam_value(lam_ref, lam_init)
    a = _softmax(_scores(q0[...], k0[...])) - lam * _softmax(_scores(q1[...], k1[...]))
    o = jnp.dot(a.astype(BF16), va[...].astype(BF16), preferred_element_type=F32)
    oa_ref[...] = _subln(o, g_ref, lam_init).astype(oa_ref.dtype)
    for t in range(2):
        cols = slice(t * HEAD_DIM, (t + 1) * HEAD_DIM)
        p = _softmax(_scores(qb[:, cols], kb[:, cols]))
        ob_ref[:, cols] = jnp.dot(p.astype(BF16), vb[:, cols].astype(BF16),
                                  preferred_element_type=F32).astype(ob_ref.dtype)


def _prompt_attn(qkv, lam_vecs, subln_g, lam_init, seq_len):
    rows = qkv.shape[0]
    nb = rows // seq_len
    sec128 = SECTION // HEAD_DIM
    sec256 = SECTION // (2 * HEAD_DIM)

    def blk(width, col0):
        return pl.BlockSpec((seq_len, width), lambda b, h: (b, col0 + h))

    half = sec128 // 2
    in_specs = [
        pl.BlockSpec((4, HEAD_DIM), lambda b, h: (0, 0)),
        blk(HEAD_DIM, 0), blk(HEAD_DIM, half),
        blk(HEAD_DIM, sec128), blk(HEAD_DIM, sec128 + half),
        blk(2 * HEAD_DIM, 2 * sec256),
        blk(2 * HEAD_DIM, 3 * sec256),
        blk(2 * HEAD_DIM, 4 * sec256),
        blk(2 * HEAD_DIM, 5 * sec256),
        pl.BlockSpec((1, 2 * HEAD_DIM), lambda b, h: (0, 0)),
    ]
    out_spec = pl.BlockSpec((seq_len, 2 * HEAD_DIM), lambda b, h: (b, h))
    out = jax.ShapeDtypeStruct((rows, SECTION), BF16)
    return pl.pallas_call(
        functools.partial(_prompt_attn_body, lam_init=lam_init),
        grid=(nb, N_HEADS_A),
        in_specs=in_specs,
        out_specs=[out_spec, out_spec],
        out_shape=[out, out],
        compiler_params=_params(("parallel", "parallel")),
        name="prompt_attn",
    )(lam_vecs, qkv, qkv, qkv, qkv, qkv, qkv, qkv, qkv, subln_g)


DIFF_KEY_CHUNK = 1024


def _latent_diff_body(lam_ref, q0, q1, k0, k1, v, kc0, kc1, vc, g_ref, o_ref,
                      *, lam_init):
    lam = _lam_value(lam_ref, lam_init)
    v_ctx = vc[...].astype(BF16)
    n_lat = k0.shape[0]

    def attend(q, k, kc):
        qv = q[...]
        s = _scores(qv, kc[...])
        m = jnp.max(s, axis=-1, keepdims=True)
        e = jnp.exp2(s - m)
        denom = jnp.sum(e, axis=-1, keepdims=True)
        acc = jnp.dot(e.astype(BF16), v_ctx, preferred_element_type=F32)
        for c in range(n_lat // DIFF_KEY_CHUNK):
            keys = slice(c * DIFF_KEY_CHUNK, (c + 1) * DIFF_KEY_CHUNK)
            s = _scores(qv, k[keys, :])
            m_new = jnp.maximum(m, jnp.max(s, axis=-1, keepdims=True))
            alpha = jnp.exp2(m - m_new)
            e = jnp.exp2(s - m_new)
            denom = denom * alpha + jnp.sum(e, axis=-1, keepdims=True)
            acc = acc * alpha + jnp.dot(e.astype(BF16), v[keys, :], preferred_element_type=F32)
            m = m_new
        return acc / denom

    o = attend(q0, k0, kc0) - lam * attend(q1, k1, kc1)
    o_ref[...] = _subln(o, g_ref, lam_init).astype(o_ref.dtype)


def _latent_diff_attn(qkv, cache_k, cache_v, lam_vecs, subln_g, lam_init,
                      seq_len, ctx_len):
    rows = qkv.shape[0]
    nb = rows // seq_len
    tq = 256
    nq = seq_len // tq
    sec128 = SECTION // HEAD_DIM
    sec256 = SECTION // (2 * HEAD_DIM)
    half = sec128 // 2

    def qblk(col0):
        return pl.BlockSpec((tq, HEAD_DIM), lambda b, h, i: (b * nq + i, col0 + h))

    def kvblk(n, width, col0):
        return pl.BlockSpec((n, width), lambda b, h, i: (b, col0 + h))

    in_specs = [
        pl.BlockSpec((4, HEAD_DIM), lambda b, h, i: (0, 0)),
        qblk(0), qblk(half),
        kvblk(seq_len, HEAD_DIM, sec128), kvblk(seq_len, HEAD_DIM, sec128 + half),
        kvblk(seq_len, 2 * HEAD_DIM, 2 * sec256),
        kvblk(ctx_len, HEAD_DIM, 0), kvblk(ctx_len, HEAD_DIM, half),
        kvblk(ctx_len, 2 * HEAD_DIM, 0),
        pl.BlockSpec((1, 2 * HEAD_DIM), lambda b, h, i: (0, 0)),
    ]
    return pl.pallas_call(
        functools.partial(_latent_diff_body, lam_init=lam_init),
        grid=(nb, N_HEADS_A, nq),
        in_specs=in_specs,
        out_specs=pl.BlockSpec((tq, 2 * HEAD_DIM), lambda b, h, i: (b * nq + i, h)),
        out_shape=jax.ShapeDtypeStruct((rows, SECTION), BF16),
        compiler_params=_params(("parallel", "parallel", "arbitrary")),
        name="latent_diff_attn",
    )(lam_vecs, qkv, qkv, qkv, qkv, qkv, cache_k, cache_k, cache_v, subln_g)


NA_Q_ROWS = 8
NA_K_ROWS = 16


def _na_band_start(i, grid_rows):
    return jnp.clip(i * NA_Q_ROWS - NA_ROWS // 2, 0, grid_rows - NA_K_ROWS)


NA_D_MIN = -NA_Q_ROWS
NA_D_MAX = NA_K_ROWS + NA_ROWS - 2


def _na_pair_tables(rpb):
    n_h, n_dr, n_dc = rpb.shape
    u = jnp.concatenate([rpb[..., NA_COLS - 1:],
                         jnp.zeros((n_h, n_dr, LANE - n_dc), F32),
                         rpb[..., :NA_COLS - 1]], axis=-1)
    skew = jnp.tile(u, (1, 1, GRID_W))[..., :GRID_W * (LANE - 1)]
    toep = skew.reshape(n_h, n_dr, GRID_W, LANE - 1)[..., :GRID_W]
    cols = jnp.arange(GRID_W)
    cstart = jnp.clip(cols - NA_COLS // 2, 0, GRID_W - NA_COLS)
    col_ok = (cols[None, :] >= cstart[:, None]) & (cols[None, :] < cstart[:, None] + NA_COLS)
    toep = jnp.where(col_ok, toep, MASK_VALUE)
    ext = jnp.pad(toep, ((0, 0), (-NA_D_MIN, NA_D_MAX + 1 - n_dr), (0, 0), (0, 0)),
                  constant_values=MASK_VALUE)
    return jnp.concatenate([ext[:, :-1], ext[:, 1:]], axis=-1)


def _latent_na_body(q, k, v, kc, vc, tab_ref, o_ref, bias_ref, *, grid_rows):
    i = pl.program_id(2)
    n_blocks = grid_rows // NA_Q_ROWS
    row0 = i * NA_Q_ROWS
    band0 = _na_band_start(i, grid_rows)

    @pl.when(jnp.logical_or(i <= 1, i == n_blocks - 1))
    def _():
        lane = lax.broadcasted_iota(jnp.int32, (GRID_W, LANE), 1)
        for rq in range(NA_Q_ROWS):
            r = row0 + rq
            rs = jnp.clip(r - NA_ROWS // 2, 0, grid_rows - NA_ROWS)

            def penalty(kr):
                ok = jnp.logical_and(kr >= rs, kr < rs + NA_ROWS)
                return jnp.where(ok, 0.0, MASK_VALUE)

            for p in range(NA_K_ROWS // 2):
                kr = band0 + 2 * p
                d = kr - r + NA_ROWS - 1
                pen = jnp.where(lane < GRID_W, penalty(kr), penalty(kr + 1))
                bias_ref[rq * GRID_W:(rq + 1) * GRID_W, p * LANE:(p + 1) * LANE] = (
                    tab_ref[d - NA_D_MIN] * LOG2E + pen)

    start = pl.multiple_of(band0 * GRID_W, GRID_W)
    band = pl.ds(start, NA_K_ROWS * GRID_W)
    e_loc, e_ctx, denom = _joint_exp(_scores(q[...], k[band, :]) + bias_ref[...],
                                     _scores(q[...], kc[...]))
    o = (jnp.dot(e_loc, v[band, :], preferred_element_type=F32)
         + jnp.dot(e_ctx, vc[...].astype(BF16), preferred_element_type=F32))
    o_ref[...] = (o / denom).astype(o_ref.dtype)


def _latent_na_attn(qkv, cache_k, cache_v, rpb, seq_len, ctx_len):
    rows = qkv.shape[0]
    nb = rows // seq_len
    grid_rows = seq_len // GRID_W
    n_blocks = grid_rows // NA_Q_ROWS
    tq = NA_Q_ROWS * GRID_W
    sec128 = SECTION // HEAD_DIM
    tables = _na_pair_tables(rpb)

    def kvblk(n, col0):
        return pl.BlockSpec((n, HEAD_DIM), lambda h, b, i: (b, col0 + h))

    in_specs = [
        pl.BlockSpec((tq, HEAD_DIM), lambda h, b, i: (b * n_blocks + i, 3 * sec128 + h)),
        kvblk(seq_len, 4 * sec128), kvblk(seq_len, 5 * sec128),
        kvblk(ctx_len, 0), kvblk(ctx_len, 0),
        pl.BlockSpec((None,) + tables.shape[1:], lambda h, b, i: (h, 0, 0, 0)),
    ]
    return pl.pallas_call(
        functools.partial(_latent_na_body, grid_rows=grid_rows),
        grid=(N_HEADS_B, nb, n_blocks),
        in_specs=in_specs,
        out_specs=pl.BlockSpec((tq, HEAD_DIM), lambda h, b, i: (b * n_blocks + i, h)),
        out_shape=jax.ShapeDtypeStruct((rows, SECTION), BF16),
        scratch_shapes=[pltpu.VMEM((tq, NA_K_ROWS * GRID_W), F32)],
        compiler_params=_params(("arbitrary", "arbitrary", "arbitrary")),
        name="latent_na_attn",
    )(qkv, qkv, qkv, cache_k, cache_v, tables)


def _spatial_gate_body(u_ref, v_ref, g_ref, ws_ref, bs_ref, o_ref):
    vn = (_rms(v_ref[...]) * g_ref[...]).astype(BF16)
    width = vn.shape[1] // N_GROUPS
    for g in range(N_GROUPS):
        cols = slice(g * width, (g + 1) * width)
        sv = jnp.dot(ws_ref[g].astype(BF16), vn[:, cols],
                     preferred_element_type=F32) + bs_ref[:, g:g + 1]
        o_ref[:, cols] = (u_ref[:, cols] * sv).astype(o_ref.dtype)


def _spatial_gate(z, vnorm_g, w_sp, b_sp, layer):
    rows, two_d = z.shape
    d = two_d // 2
    return pl.pallas_call(
        _spatial_gate_body,
        grid=(rows // CHUNK,),
        in_specs=[pl.BlockSpec((CHUNK, d), lambda n: (n, 0)),
                  pl.BlockSpec((CHUNK, d), lambda n: (n, 1)),
                  pl.BlockSpec((None, 1, d), lambda n: (layer, 0, 0)),
                  pl.BlockSpec((None, N_GROUPS, CHUNK, CHUNK), lambda n: (layer, 0, 0, 0)),
                  pl.BlockSpec((None, CHUNK, N_GROUPS), lambda n: (layer, 0, 0))],
        out_specs=pl.BlockSpec((CHUNK, d), lambda n: (n, 0)),
        out_shape=jax.ShapeDtypeStruct((rows, d), BF16),
        compiler_params=_params(("parallel",)),
        name="spatial_gate",
    )(z, z, vnorm_g.reshape(vnorm_g.shape[0], 1, d), w_sp, jnp.swapaxes(b_sp, 1, 2))


FFN_TN = 256
HALO = 16
SUB = 8
FFN_CHUNKS = (256, 256, 256, 128, 128)


def _ffn_front_body(x_ref, xp_ref, xn_ref, wg_ref, wu_ref, cwg_ref, cwu_ref,
                    cbg_ref, cbu_ref, o_ref, xs_ref, ag_ref, au_ref, *, seq_len):
    tm = x_ref.shape[0]
    i = pl.program_id(0)

    @pl.when(pl.program_id(1) == 0)
    def _():
        xs_ref[0:HALO, :] = xp_ref[...]
        xs_ref[HALO:HALO + tm, :] = x_ref[...]
        xs_ref[HALO + tm:, :] = xn_ref[...]

    starts = [sum(FFN_CHUNKS[:c]) for c in range(len(FFN_CHUNKS) + 1)]
    n_chunks = len(FFN_CHUNKS)

    def project(c):
        lo = 0 if c == 0 else HALO + starts[c]
        hi = tm + 2 * HALO if c == n_chunks - 1 else HALO + starts[c + 1]
        xs = xs_ref[lo:hi, :]
        ag_ref[lo:hi, :] = jnp.dot(xs, wg_ref[...], preferred_element_type=F32)
        au_ref[lo:hi, :] = jnp.dot(xs, wu_ref[...], preferred_element_type=F32)

    def conv(acc_ref, cw_ref, cb_ref, c, at_start, at_end):
        n = FFN_CHUNKS[c]
        r0 = HALO + starts[c]
        win = acc_ref[r0 - SUB:r0 + n + SUB, :]
        mid = slice(SUB, SUB + n)
        x_prev = pltpu.roll(win, 1, 0)[mid]
        x_next = pltpu.roll(win, n + 2 * SUB - 1, 0)[mid]
        row = lax.broadcasted_iota(jnp.int32, (SUB, 1), 0)
        head = jnp.where(jnp.logical_and(at_start, row == 0), 0.0, x_prev[:SUB])
        tail = jnp.where(jnp.logical_and(at_end, row == SUB - 1), 0.0, x_next[n - SUB:])
        x_prev = jnp.concatenate([head, x_prev[SUB:]], axis=0)
        x_next = jnp.concatenate([x_next[:n - SUB], tail], axis=0)
        return (cw_ref[0:1, :] * x_prev + cw_ref[1:2, :] * win[mid]
                + cw_ref[2:3, :] * x_next + cb_ref[...])

    def gate(c):
        first = i * tm + starts[c]
        at_start = first % seq_len == 0
        at_end = (first + FFN_CHUNKS[c]) % seq_len == 0
        g = conv(ag_ref, cwg_ref, cbg_ref, c, at_start, at_end)
        u = conv(au_ref, cwu_ref, cbu_ref, c, at_start, at_end)
        o_ref[starts[c]:starts[c + 1], :] = (g * jax.nn.sigmoid(g) * u).astype(o_ref.dtype)

    project(0)
    for c in range(n_chunks):
        if c + 1 < n_chunks:
            project(c + 1)
        gate(c)


def _ffn_front(h, w_up, conv_w, conv_b, layer, group):
    rows, d = h.shape
    two_f = w_up.shape[-1]
    d_ff = two_f // 2
    tm, tn = MM_TM, FFN_TN
    nj = d_ff // tn
    starts = {sum(FFN_CHUNKS[:c]) for c in range(len(FFN_CHUNKS))}
    assert sum(FFN_CHUNKS) == tm and d_ff % tn == 0
    assert tm % group.seq_len == 0 or group.seq_len % tm == 0
    assert all(s in starts for s in range(0, tm, group.seq_len))
    halo_blocks = rows // HALO

    def w(off):
        return pl.BlockSpec((None, d, tn), lambda i, j: (layer, 0, off + j))

    def cw(off):
        return pl.BlockSpec((None, 3, tn), lambda i, j: (layer, 0, off + j))

    def cb(off):
        return pl.BlockSpec((None, 1, tn), lambda i, j: (layer, 0, off + j))

    in_specs = [
        pl.BlockSpec((tm, d), lambda i, j: (i, 0)),
        pl.BlockSpec((HALO, d), lambda i, j: (jnp.maximum(i * (tm // HALO) - 1, 0), 0)),
        pl.BlockSpec((HALO, d), lambda i, j: (jnp.minimum((i + 1) * (tm // HALO), halo_blocks - 1), 0)),
        w(0), w(nj), cw(0), cw(nj), cb(0), cb(nj),
    ]
    cb3 = conv_b.reshape(conv_b.shape[0], 1, two_f)
    return pl.pallas_call(
        functools.partial(_ffn_front_body, seq_len=group.seq_len),
        grid=(rows // tm, nj),
        in_specs=in_specs,
        out_specs=pl.BlockSpec((tm, tn), lambda i, j: (i, j)),
        out_shape=jax.ShapeDtypeStruct((rows, d_ff), BF16),
        scratch_shapes=[pltpu.VMEM((tm + 2 * HALO, d), BF16),
                        pltpu.VMEM((tm + 2 * HALO, tn), F32),
                        pltpu.VMEM((tm + 2 * HALO, tn), F32)],
        compiler_params=_params(("arbitrary", "arbitrary")),
        name="ffn_front",
    )(h, h, h, w_up, w_up, conv_w, conv_w, cb3, cb3)


def kernel(x_prompt, x_sample, cache_a_k, cache_a_v, cache_b_k, cache_b_v, c, c_ctx,
           w_ada, b_ada, norm1_g, norm2_g,
           w_in_even, qn_a, kn_a, lam_q1, lam_k1, lam_q2, lam_k2, subln_g, qn_b, kn_b, rpb, w_out_even,
           w_in_odd, vnorm_g, w_sp, b_sp, w_out_odd,
           w_up, conv_w, conv_b, w_down):
    batch, seq, d = x_prompt.shape
    dec_batch, dec_seq, _ = x_sample.shape
    past_len = cache_a_k.shape[2]
    depth = w_ada.shape[0]
    n_even = w_in_even.shape[0]

    groups = (_Group(batch * seq, seq, 0), _Group(dec_batch * dec_seq, dec_seq, 1))
    xs = [x_prompt.reshape(batch * seq, d), x_sample.reshape(dec_batch * dec_seq, d)]

    cond8 = jnp.concatenate([c_ctx[None], c, jnp.zeros((8 - 1 - dec_batch, d), F32)])

    w_in_even_b = w_in_even.astype(BF16)
    w_out_even_b = w_out_even.astype(BF16)
    w_in_odd_b = w_in_odd.astype(BF16)
    w_out_odd_b = w_out_odd.astype(BF16)
    w_up_b = w_up.astype(BF16)
    w_down_b = w_down.astype(BF16)

    prompt_qkv = []
    for layer in range(depth):
        mod = _adaln(cond8, w_ada, b_ada, layer).reshape(8, 6, 1, d)
        if layer % 2 == 0:
            e = layer // 2
            lam_init = 0.8 - 0.6 * math.exp(-0.3 * layer)
            lam_vecs = jnp.stack([lam_q1[e], lam_k1[e], lam_q2[e], lam_k2[e]])
            ones = jnp.ones((SECTION,), F32)
            reps = SECTION // HEAD_DIM
            gains = jnp.concatenate([jnp.tile(qn_a[e] * Q_PRESCALE, reps), jnp.tile(kn_a[e], reps), ones,
                                     jnp.tile(qn_b[e] * Q_PRESCALE, reps), jnp.tile(kn_b[e], reps), ones])[None]
            sub_g = subln_g[e][None]
            mixed = []
            for gi, group in enumerate(groups):
                h = _norm_mod(xs[gi], norm1_g, layer, mod, 0, group)
                if gi == 0:
                    qkv = _qkv_proj(h, w_in_even_b, e, gains, group, rope=False, out_dtype=F32)
                    prompt_qkv.append(qkv)
                    oa, ob = _prompt_attn(qkv, lam_vecs, sub_g, lam_init, group.seq_len)
                else:
                    qkv = _qkv_proj(h, w_in_even_b, e, gains, group, rope=True, out_dtype=BF16)
                    ck_a = cache_a_k[:, e].reshape(dec_batch * past_len, SECTION)
                    cv_a = cache_a_v[:, e].reshape(dec_batch * past_len, SECTION)
                    ck_b = cache_b_k[:, e].reshape(dec_batch * past_len, SECTION)
                    cv_b = cache_b_v[:, e].reshape(dec_batch * past_len, SECTION)
                    oa = _latent_diff_attn(qkv, ck_a, cv_a, lam_vecs, sub_g, lam_init,
                                           group.seq_len, past_len)
                    ob = _latent_na_attn(qkv, ck_b, cv_b, rpb[e], group.seq_len, past_len)
                mixed.append(jnp.concatenate([oa, ob], axis=-1))
            for gi, group in enumerate(groups):
                xs[gi] = _matmul_resid(mixed[gi], w_out_even_b, e, xs[gi], mod, 2, group,
                                       name="even_out_proj")
        else:
            o = layer // 2
            for gi, group in enumerate(groups):
                h = _norm_mod(xs[gi], norm1_g, layer, mod, 0, group)
                z = _matmul(h, w_in_odd_b, o, _gelu_epilogue, (), [], F32, name="odd_in_proj")
                gated = _spatial_gate(z, vnorm_g, w_sp, b_sp, o)
                xs[gi] = _matmul_resid(gated, w_out_odd_b, o, xs[gi], mod, 2, group,
                                       name="odd_out_proj")
        for gi, group in enumerate(groups):
            h = _norm_mod(xs[gi], norm2_g, layer, mod, 3, group)
            act = _ffn_front(h, w_up_b, conv_w, conv_b, layer, group)
            half = _matmul(act, w_down_b, layer, _plain_epilogue, (), [], F32,
                           k_parts=2, k_part=0, name="ffn_down_lo")
            xs[gi] = _matmul_resid(act, w_down_b, layer, xs[gi], mod, 5, group, partial_sum=half,
                                   k_parts=2, k_part=1, name="ffn_down_hi")

    def state(col0, shape):
        parts = [q[:, col0:col0 + SECTION].reshape((batch, seq) + shape) for q in prompt_qkv]
        return jnp.stack(parts, axis=1)

    h_a, h_b = N_HEADS_A, N_HEADS_B
    return (xs[0].reshape(batch, seq, d),
            xs[1].reshape(dec_batch, dec_seq, d),
            state(1 * SECTION, (2, h_a, HEAD_DIM)),
            state(2 * SECTION, (h_a, 2 * HEAD_DIM)),
            state(4 * SECTION, (h_b, HEAD_DIM)),
            state(5 * SECTION, (h_b, HEAD_DIM)))
```

```python
import functools
import math

import jax
import jax.numpy as jnp
from jax import lax
from jax.experimental import pallas as pl
from jax.experimental.pallas import tpu as pltpu

F32 = jnp.float32
BF16 = jnp.bfloat16

NORM_EPS = 1e-6
ROPE_BASE = 10000.0
GRID_W = 64
HEAD_DIM = 128
N_HEADS_A = 8
N_HEADS_B = 16
SECTION = 2048
NA_ROWS = 8
NA_COLS = 16
CHUNK = 128
N_GROUPS = 8
MASK_VALUE = -1e30
LOG2E = math.log2(math.e)
Q_PRESCALE = HEAD_DIM ** -0.5 * LOG2E
LANE = 128
VMEM_LIMIT = 56 * 1024 * 1024

_NT = (((1,), (1,)), ((), ()))


def _params(semantics):
    return pltpu.CompilerParams(dimension_semantics=semantics,
                                vmem_limit_bytes=VMEM_LIMIT)


def _rms(x):
    return x * lax.rsqrt(jnp.mean(x * x, axis=-1, keepdims=True) + NORM_EPS)


def _adaln_body(c_ref, w_ref, b_ref, o_ref):
    c = c_ref[...]
    s = c * jax.nn.sigmoid(c)
    o_ref[...] = jnp.dot(s.astype(BF16), w_ref[...].astype(BF16),
                         preferred_element_type=F32) + b_ref[...]


def _adaln(cond8, w_ada, b_ada, layer):
    _, d, n = w_ada.shape
    tn = 512
    return pl.pallas_call(
        _adaln_body,
        grid=(n // tn,),
        in_specs=[pl.BlockSpec((8, d), lambda j: (0, 0)),
                  pl.BlockSpec((None, d, tn), lambda j: (layer, 0, j)),
                  pl.BlockSpec((None, 1, tn), lambda j: (layer, 0, j))],
        out_specs=pl.BlockSpec((8, tn), lambda j: (0, j)),
        out_shape=jax.ShapeDtypeStruct((8, n), F32),
        compiler_params=_params(("parallel",)),
        name="adaln",
    )(cond8, w_ada, b_ada.reshape(b_ada.shape[0], 1, n))


class _Group:
    def __init__(self, rows, seq_len, cond_base):
        self.rows = rows
        self.seq_len = seq_len
        self.cond_base = cond_base

    def cond(self, row0):
        if self.cond_base == 0:
            return 0
        return self.cond_base + row0 // self.seq_len


def _mod_spec(group, tm, tn, which, col_of):
    return pl.BlockSpec(
        (None, None, 1, tn),
        lambda *ids: (group.cond(ids[0] * tm), which, 0, col_of(*ids)))


def _norm_mod_body(x_ref, g_ref, sh_ref, sc_ref, o_ref):
    y = _rms(x_ref[...]) * g_ref[...]
    o_ref[...] = (y * (1 + sc_ref[...]) + sh_ref[...]).astype(o_ref.dtype)


def _norm_mod(x, gains, layer, mod, shift_idx, group):
    rows, d = x.shape
    tm = 256
    return pl.pallas_call(
        _norm_mod_body,
        grid=(rows // tm,),
        in_specs=[pl.BlockSpec((tm, d), lambda i: (i, 0)),
                  pl.BlockSpec((None, 1, d), lambda i: (layer, 0, 0)),
                  _mod_spec(group, tm, d, shift_idx, lambda i: 0),
                  _mod_spec(group, tm, d, shift_idx + 1, lambda i: 0)],
        out_specs=pl.BlockSpec((tm, d), lambda i: (i, 0)),
        out_shape=jax.ShapeDtypeStruct((rows, d), BF16),
        compiler_params=_params(("parallel",)),
        name="norm_mod",
    )(x, gains.reshape(gains.shape[0], 1, d), mod, mod)


MM_TM = 1024
MM_TN = 512
MM_CHUNK = 256


def _mm_body(*refs, n_extra, epilogue):
    x_ref, w_ref = refs[0], refs[1]
    extras = refs[2:2 + n_extra]
    o_ref = refs[2 + n_extra]
    j = pl.program_id(1)
    w = w_ref[...].astype(BF16)
    for c in range(x_ref.shape[0] // MM_CHUNK):
        rows = slice(c * MM_CHUNK, (c + 1) * MM_CHUNK)
        acc = jnp.dot(x_ref[rows, :], w, preferred_element_type=F32)
        epilogue(acc, extras, o_ref, j, rows)


def _matmul(x, w, layer, epilogue, extras, extra_specs, out_dtype, *,
            k_parts=1, k_part=0, name):
    rows, kdim = x.shape
    n = w.shape[-1]
    tm, tn = MM_TM, MM_TN
    tk = kdim // k_parts
    body = functools.partial(_mm_body, n_extra=len(extras), epilogue=epilogue)
    return pl.pallas_call(
        body,
        grid=(rows // tm, n // tn),
        in_specs=[pl.BlockSpec((tm, tk), lambda i, j: (i, k_part)),
                  pl.BlockSpec((None, tk, tn), lambda i, j: (layer, k_part, j)),
                  *extra_specs],
        out_specs=pl.BlockSpec((tm, tn), lambda i, j: (i, j)),
        out_shape=jax.ShapeDtypeStruct((rows, n), out_dtype),
        compiler_params=_params(("parallel", "arbitrary")),
        name=name,
    )(x, w, *extras)


def _plain_epilogue(acc, extras, o_ref, j, rows):
    o_ref[rows, :] = acc.astype(o_ref.dtype)


def _gelu_epilogue(acc, extras, o_ref, j, rows):
    o_ref[rows, :] = (0.5 * acc * (1 + lax.erf(acc * math.sqrt(0.5)))).astype(o_ref.dtype)


def _resid_epilogue(acc, extras, o_ref, j, rows):
    x_ref, gate_ref = extras[:2]
    if len(extras) == 3:
        acc = extras[2][rows, :] + acc
    o_ref[rows, :] = x_ref[rows, :] + gate_ref[...] * acc


def _matmul_resid(h, w, layer, x, mod, gate_idx, group, *, partial_sum=None,
                  k_parts=1, k_part=0, name):
    tile = pl.BlockSpec((MM_TM, MM_TN), lambda i, j: (i, j))
    specs = [tile, _mod_spec(group, MM_TM, MM_TN, gate_idx, lambda i, j: j)]
    extras = (x, mod)
    if partial_sum is not None:
        specs.append(tile)
        extras += (partial_sum,)
    return _matmul(h, w, layer, _resid_epilogue, extras, specs, F32,
                   k_parts=k_parts, k_part=k_part, name=name)


def _qkv_epilogue(acc, extras, o_ref, j, rows, *, rope):
    g_ref = extras[0]
    sec = (j * MM_TN) // SECTION
    is_v = jnp.logical_or(sec == 2, sec == 5)
    if rope:
        is_rope = sec < 2
        cos = jnp.where(is_rope, extras[1][rows, :], 1.0)
        sin_lo = jnp.where(is_rope, extras[2][rows, :], 0.0)
        sin_hi = jnp.where(is_rope, extras[3][rows, :], 0.0)
    for c in range(MM_TN // LANE):
        cols = slice(c * LANE, (c + 1) * LANE)
        a = acc[:, cols]
        y = jnp.where(is_v, a, _rms(a) * g_ref[:, cols])
        if rope:
            y = (y * cos + pltpu.roll(y, LANE - 32, 1) * sin_lo
                 + pltpu.roll(y, 32, 1) * sin_hi)
        o_ref[rows, cols] = y.astype(o_ref.dtype)


def _rope_tables(n_tok):
    quarter = HEAD_DIM // 4
    t = jnp.arange(n_tok)
    inv = ROPE_BASE ** (-jnp.arange(quarter, dtype=F32) / quarter)

    def ang(p):
        a = p.astype(F32)[:, None] * inv[None, :]
        return jnp.concatenate([a, a], axis=-1)

    angles = jnp.concatenate([ang(t // GRID_W), ang(t % GRID_W)], axis=-1)
    cos, sin = jnp.cos(angles), jnp.sin(angles)
    low = (jnp.arange(HEAD_DIM) % (2 * quarter)) < quarter
    return cos, jnp.where(low, -sin, 0.0), jnp.where(low, 0.0, sin)


def _qkv_proj(h, w, layer, gains, group, *, rope, out_dtype):
    extras = [gains]
    specs = [pl.BlockSpec((1, MM_TN), lambda i, j: (0, j))]
    if rope:
        tiles_per_seq = group.seq_len // MM_TM
        tab = pl.BlockSpec((MM_TM, HEAD_DIM), lambda i, j: (i % tiles_per_seq, 0))
        extras += list(_rope_tables(group.seq_len))
        specs += [tab, tab, tab]
    epi = functools.partial(_qkv_epilogue, rope=rope)
    return _matmul(h, w, layer, epi, tuple(extras), specs, out_dtype, name="qkv_proj")


def _lam_value(lam_ref, lam_init):
    v = lam_ref[...]
    t1 = jnp.sum(v[0:1] * v[1:2], axis=-1, keepdims=True)
    t2 = jnp.sum(v[2:3] * v[3:4], axis=-1, keepdims=True)
    return jnp.exp(t1) - jnp.exp(t2) + lam_init


def _scores(q, k):
    return lax.dot_general(q.astype(BF16), k.astype(BF16), _NT,
                           preferred_element_type=F32)


def _softmax(s):
    e = jnp.exp2(s - jnp.max(s, axis=-1, keepdims=True))
    return e / jnp.sum(e, axis=-1, keepdims=True)


def _joint_exp(s_a, s_b):
    m = jnp.maximum(jnp.max(s_a, axis=-1, keepdims=True),
                    jnp.max(s_b, axis=-1, keepdims=True))
    e_a = jnp.exp2(s_a - m)
    e_b = jnp.exp2(s_b - m)
    denom = (jnp.sum(e_a, axis=-1, keepdims=True)
             + jnp.sum(e_b, axis=-1, keepdims=True))
    return e_a.astype(BF16), e_b.astype(BF16), denom


def _subln(o, g_ref, lam_init):
    return _rms(o) * g_ref[...] * (1.0 - lam_init)


def _prompt_attn_body(lam_ref, q0, q1, k0, k1, va, qb, kb, vb, g_ref,
                      oa_ref, ob_ref, *, lam_init):
    lam = _lam_value(lam_ref, lam_init)
    a = _softmax(_scores(q0[...], k0[...])) - lam * _softmax(_scores(q1[...], k1[...]))
    o = jnp.dot(a.astype(BF16), va[...].astype(BF16), preferred_element_type=F32)
    oa_ref[...] = _subln(o, g_ref, lam_init).astype(oa_ref.dtype)
    for t in range(2):
        cols = slice(t * HEAD_DIM, (t + 1) * HEAD_DIM)
        p = _softmax(_scores(qb[:, cols], kb[:, cols]))
        ob_ref[:, cols] = jnp.dot(p.astype(BF16), vb[:, cols].astype(BF16),
                                  preferred_element_type=F32).astype(ob_ref.dtype)


def _prompt_attn(qkv, lam_vecs, subln_g, lam_init, seq_len):
    rows = qkv.shape[0]
    nb = rows // seq_len
    sec128 = SECTION // HEAD_DIM
    sec256 = SECTION // (2 * HEAD_DIM)

    def blk(width, col0):
        return pl.BlockSpec((seq_len, width), lambda b, h: (b, col0 + h))

    half = sec128 // 2
    in_specs = [
        pl.BlockSpec((4, HEAD_DIM), lambda b, h: (0, 0)),
        blk(HEAD_DIM, 0), blk(HEAD_DIM, half),
        blk(HEAD_DIM, sec128), blk(HEAD_DIM, sec128 + half),
        blk(2 * HEAD_DIM, 2 * sec256),
        blk(2 * HEAD_DIM, 3 * sec256),
        blk(2 * HEAD_DIM, 4 * sec256),
        blk(2 * HEAD_DIM, 5 * sec256),
        pl.BlockSpec((1, 2 * HEAD_DIM), lambda b, h: (0, 0)),
    ]
    out_spec = pl.BlockSpec((seq_len, 2 * HEAD_DIM), lambda b, h: (b, h))
    out = jax.ShapeDtypeStruct((rows, SECTION), BF16)
    return pl.pallas_call(
        functools.partial(_prompt_attn_body, lam_init=lam_init),
        grid=(nb, N_HEADS_A),
        in_specs=in_specs,
        out_specs=[out_spec, out_spec],
        out_shape=[out, out],
        compiler_params=_params(("parallel", "parallel")),
        name="prompt_attn",
    )(lam_vecs, qkv, qkv, qkv, qkv, qkv, qkv, qkv, qkv, subln_g)


DIFF_KEY_CHUNK = 1024


def _latent_diff_body(lam_ref, q0, q1, k0, k1, v, kc0, kc1, vc, g_ref, o_ref,
                      *, lam_init):
    lam = _lam_value(lam_ref, lam_init)
    v_ctx = vc[...].astype(BF16)
    n_lat = k0.shape[0]

    def attend(q, k, kc):
        qv = q[...]
        s = _scores(qv, kc[...])
        m = jnp.max(s, axis=-1, keepdims=True)
        e = jnp.exp2(s - m)
        denom = jnp.sum(e, axis=-1, keepdims=True)
        acc = jnp.dot(e.astype(BF16), v_ctx, preferred_element_type=F32)
        for c in range(n_lat // DIFF_KEY_CHUNK):
            keys = slice(c * DIFF_KEY_CHUNK, (c + 1) * DIFF_KEY_CHUNK)
            s = _scores(qv, k[keys, :])
            m_new = jnp.maximum(m, jnp.max(s, axis=-1, keepdims=True))
            alpha = jnp.exp2(m - m_new)
            e = jnp.exp2(s - m_new)
            denom = denom * alpha + jnp.sum(e, axis=-1, keepdims=True)
            acc = acc * alpha + jnp.dot(e.astype(BF16), v[keys, :], preferred_element_type=F32)
            m = m_new
        return acc / denom

    o = attend(q0, k0, kc0) - lam * attend(q1, k1, kc1)
    o_ref[...] = _subln(o, g_ref, lam_init).astype(o_ref.dtype)


def _latent_diff_attn(qkv, cache_k, cache_v, lam_vecs, subln_g, lam_init,
                      seq_len, ctx_len):
    rows = qkv.shape[0]
    nb = rows // seq_len
    tq = 256
    nq = seq_len // tq
    sec128 = SECTION // HEAD_DIM
    sec256 = SECTION // (2 * HEAD_DIM)
    half = sec128 // 2

    def qblk(col0):
        return pl.BlockSpec((tq, HEAD_DIM), lambda b, h, i: (b * nq + i, col0 + h))

    def kvblk(n, width, col0):
        return pl.BlockSpec((n, width), lambda b, h, i: (b, col0 + h))

    in_specs = [
        pl.BlockSpec((4, HEAD_DIM), lambda b, h, i: (0, 0)),
        qblk(0), qblk(half),
        kvblk(seq_len, HEAD_DIM, sec128), kvblk(seq_len, HEAD_DIM, sec128 + half),
        kvblk(seq_len, 2 * HEAD_DIM, 2 * sec256),
        kvblk(ctx_len, HEAD_DIM, 0), kvblk(ctx_len, HEAD_DIM, half),
        kvblk(ctx_len, 2 * HEAD_DIM, 0),
        pl.BlockSpec((1, 2 * HEAD_DIM), lambda b, h, i: (0, 0)),
    ]
    return pl.pallas_call(
        functools.partial(_latent_diff_body, lam_init=lam_init),
        grid=(nb, N_HEADS_A, nq),
        in_specs=in_specs,
        out_specs=pl.BlockSpec((tq, 2 * HEAD_DIM), lambda b, h, i: (b * nq + i, h)),
        out_shape=jax.ShapeDtypeStruct((rows, SECTION), BF16),
        compiler_params=_params(("parallel", "parallel", "arbitrary")),
        name="latent_diff_attn",
    )(lam_vecs, qkv, qkv, qkv, qkv, qkv, cache_k, cache_k, cache_v, subln_g)


NA_Q_ROWS = 8
NA_K_ROWS = 16


def _na_band_start(i, grid_rows):
    return jnp.clip(i * NA_Q_ROWS - NA_ROWS // 2, 0, grid_rows - NA_K_ROWS)


NA_D_MIN = -NA_Q_ROWS
NA_D_MAX = NA_K_ROWS + NA_ROWS - 2


def _na_pair_tables(rpb):
    n_h, n_dr, n_dc = rpb.shape
    u = jnp.concatenate([rpb[..., NA_COLS - 1:],
                         jnp.zeros((n_h, n_dr, LANE - n_dc), F32),
                         rpb[..., :NA_COLS - 1]], axis=-1)
    skew = jnp.tile(u, (1, 1, GRID_W))[..., :GRID_W * (LANE - 1)]
    toep = skew.reshape(n_h, n_dr, GRID_W, LANE - 1)[..., :GRID_W]
    cols = jnp.arange(GRID_W)
    cstart = jnp.clip(cols - NA_COLS // 2, 0, GRID_W - NA_COLS)
    col_ok = (cols[None, :] >= cstart[:, None]) & (cols[None, :] < cstart[:, None] + NA_COLS)
    toep = jnp.where(col_ok, toep, MASK_VALUE)
    ext = jnp.pad(toep, ((0, 0), (-NA_D_MIN, NA_D_MAX + 1 - n_dr), (0, 0), (0, 0)),
                  constant_values=MASK_VALUE)
    return jnp.concatenate([ext[:, :-1], ext[:, 1:]], axis=-1)


def _latent_na_body(q, k, v, kc, vc, tab_ref, o_ref, bias_ref, *, grid_rows):
    i = pl.program_id(2)
    n_blocks = grid_rows // NA_Q_ROWS
    row0 = i * NA_Q_ROWS
    band0 = _na_band_start(i, grid_rows)

    @pl.when(jnp.logical_or(i <= 1, i == n_blocks - 1))
    def _():
        lane = lax.broadcasted_iota(jnp.int32, (GRID_W, LANE), 1)
        for rq in range(NA_Q_ROWS):
            r = row0 + rq
            rs = jnp.clip(r - NA_ROWS // 2, 0, grid_rows - NA_ROWS)

            def penalty(kr):
                ok = jnp.logical_and(kr >= rs, kr < rs + NA_ROWS)
                return jnp.where(ok, 0.0, MASK_VALUE)

            for p in range(NA_K_ROWS // 2):
                kr = band0 + 2 * p
                d = kr - r + NA_ROWS - 1
                pen = jnp.where(lane < GRID_W, penalty(kr), penalty(kr + 1))
                bias_ref[rq * GRID_W:(rq + 1) * GRID_W, p * LANE:(p + 1) * LANE] = (
                    tab_ref[d - NA_D_MIN] * LOG2E + pen)

    start = pl.multiple_of(band0 * GRID_W, GRID_W)
    band = pl.ds(start, NA_K_ROWS * GRID_W)
    e_loc, e_ctx, denom = _joint_exp(_scores(q[...], k[band, :]) + bias_ref[...],
                                     _scores(q[...], kc[...]))
    o = (jnp.dot(e_loc, v[band, :], preferred_element_type=F32)
         + jnp.dot(e_ctx, vc[...].astype(BF16), preferred_element_type=F32))
    o_ref[...] = (o / denom).astype(o_ref.dtype)


def _latent_na_attn(qkv, cache_k, cache_v, rpb, seq_len, ctx_len):
    rows = qkv.shape[0]
    nb = rows // seq_len
    grid_rows = seq_len // GRID_W
    n_blocks = grid_rows // NA_Q_ROWS
    tq = NA_Q_ROWS * GRID_W
    sec128 = SECTION // HEAD_DIM
    tables = _na_pair_tables(rpb)

    def kvblk(n, col0):
        return pl.BlockSpec((n, HEAD_DIM), lambda h, b, i: (b, col0 + h))

    in_specs = [
        pl.BlockSpec((tq, HEAD_DIM), lambda h, b, i: (b * n_blocks + i, 3 * sec128 + h)),
        kvblk(seq_len, 4 * sec128), kvblk(seq_len, 5 * sec128),
        kvblk(ctx_len, 0), kvblk(ctx_len, 0),
        pl.BlockSpec((None,) + tables.shape[1:], lambda h, b, i: (h, 0, 0, 0)),
    ]
    return pl.pallas_call(
        functools.partial(_latent_na_body, grid_rows=grid_rows),
        grid=(N_HEADS_B, nb, n_blocks),
        in_specs=in_specs,
        out_specs=pl.BlockSpec((tq, HEAD_DIM), lambda h, b, i: (b * n_blocks + i, h)),
        out_shape=jax.ShapeDtypeStruct((rows, SECTION), BF16),
        scratch_shapes=[pltpu.VMEM((tq, NA_K_ROWS * GRID_W), F32)],
        compiler_params=_params(("arbitrary", "arbitrary", "arbitrary")),
        name="latent_na_attn",
    )(qkv, qkv, qkv, cache_k, cache_v, tables)


def _spatial_gate_body(u_ref, v_ref, g_ref, ws_ref, bs_ref, o_ref):
    vn = (_rms(v_ref[...]) * g_ref[...]).astype(BF16)
    width = vn.shape[1] // N_GROUPS
    for g in range(N_GROUPS):
        cols = slice(g * width, (g + 1) * width)
        sv = jnp.dot(ws_ref[g].astype(BF16), vn[:, cols],
                     preferred_element_type=F32) + bs_ref[:, g:g + 1]
        o_ref[:, cols] = (u_ref[:, cols] * sv).astype(o_ref.dtype)


def _spatial_gate(z, vnorm_g, w_sp, b_sp, layer):
    rows, two_d = z.shape
    d = two_d // 2
    return pl.pallas_call(
        _spatial_gate_body,
        grid=(rows // CHUNK,),
        in_specs=[pl.BlockSpec((CHUNK, d), lambda n: (n, 0)),
                  pl.BlockSpec((CHUNK, d), lambda n: (n, 1)),
                  pl.BlockSpec((None, 1, d), lambda n: (layer, 0, 0)),
                  pl.BlockSpec((None, N_GROUPS, CHUNK, CHUNK), lambda n: (layer, 0, 0, 0)),
                  pl.BlockSpec((None, CHUNK, N_GROUPS), lambda n: (layer, 0, 0))],
        out_specs=pl.BlockSpec((CHUNK, d), lambda n: (n, 0)),
        out_shape=jax.ShapeDtypeStruct((rows, d), BF16),
        compiler_params=_params(("parallel",)),
        name="spatial_gate",
    )(z, z, vnorm_g.reshape(vnorm_g.shape[0], 1, d), w_sp, jnp.swapaxes(b_sp, 1, 2))


FFN_TN = 256
HALO = 16
SUB = 8
FFN_CHUNKS = (256, 256, 256, 128, 128)


def _ffn_front_body(x_ref, xp_ref, xn_ref, wg_ref, wu_ref, cwg_ref, cwu_ref,
                    cbg_ref, cbu_ref, o_ref, xs_ref, acc_ref, *, seq_len):
    tm = x_ref.shape[0]
    i = pl.program_id(0)

    @pl.when(pl.program_id(1) == 0)
    def _():
        xs_ref[0:HALO, :] = xp_ref[...]
        xs_ref[HALO:HALO + tm, :] = x_ref[...]
        xs_ref[HALO + tm:, :] = xn_ref[...]

    starts = [sum(FFN_CHUNKS[:c]) for c in range(len(FFN_CHUNKS) + 1)]
    n_chunks = len(FFN_CHUNKS)

    tn = wg_ref.shape[1]
    w = jnp.concatenate([wg_ref[...].astype(BF16), wu_ref[...].astype(BF16)], axis=1)

    def project(c):
        lo = 0 if c == 0 else HALO + starts[c]
        hi = tm + 2 * HALO if c == n_chunks - 1 else HALO + starts[c + 1]
        acc_ref[lo:hi, :] = jnp.dot(xs_ref[lo:hi, :], w, preferred_element_type=F32)

    def conv(col0, cw_ref, cb_ref, c, at_start, at_end):
        n = FFN_CHUNKS[c]
        r0 = HALO + starts[c]
        cols = slice(col0, col0 + tn)
        x_prev = acc_ref[r0 - 1:r0 - 1 + n, cols]
        x_next = acc_ref[r0 + 1:r0 + 1 + n, cols]
        row = lax.broadcasted_iota(jnp.int32, (SUB, 1), 0)
        head = jnp.where(jnp.logical_and(at_start, row == 0), 0.0, x_prev[:SUB])
        tail = jnp.where(jnp.logical_and(at_end, row == SUB - 1), 0.0, x_next[n - SUB:])
        x_prev = jnp.concatenate([head, x_prev[SUB:]], axis=0)
        x_next = jnp.concatenate([x_next[:n - SUB], tail], axis=0)
        return (cw_ref[0:1, :] * x_prev + cw_ref[1:2, :] * acc_ref[r0:r0 + n, cols]
                + cw_ref[2:3, :] * x_next + cb_ref[...])

    def gate(c):
        first = i * tm + starts[c]
        at_start = first % seq_len == 0
        at_end = (first + FFN_CHUNKS[c]) % seq_len == 0
        g = conv(0, cwg_ref, cbg_ref, c, at_start, at_end)
        u = conv(tn, cwu_ref, cbu_ref, c, at_start, at_end)
        o_ref[starts[c]:starts[c + 1], :] = (g * jax.nn.sigmoid(g) * u).astype(o_ref.dtype)

    project(0)
    for c in range(n_chunks):
        if c + 1 < n_chunks:
            project(c + 1)
        gate(c)


def _ffn_front(h, w_up, conv_w, conv_b, layer, group):
    rows, d = h.shape
    two_f = w_up.shape[-1]
    d_ff = two_f // 2
    tm, tn = MM_TM, FFN_TN
    nj = d_ff // tn
    starts = {sum(FFN_CHUNKS[:c]) for c in range(len(FFN_CHUNKS))}
    assert sum(FFN_CHUNKS) == tm and d_ff % tn == 0
    assert tm % group.seq_len == 0 or group.seq_len % tm == 0
    assert all(s in starts for s in range(0, tm, group.seq_len))
    halo_blocks = rows // HALO

    def w(off):
        return pl.BlockSpec((None, d, tn), lambda i, j: (layer, 0, off + j))

    def cw(off):
        return pl.BlockSpec((None, 3, tn), lambda i, j: (layer, 0, off + j))

    def cb(off):
        return pl.BlockSpec((None, 1, tn), lambda i, j: (layer, 0, off + j))

    in_specs = [
        pl.BlockSpec((tm, d), lambda i, j: (i, 0)),
        pl.BlockSpec((HALO, d), lambda i, j: (jnp.maximum(i * (tm // HALO) - 1, 0), 0)),
        pl.BlockSpec((HALO, d), lambda i, j: (jnp.minimum((i + 1) * (tm // HALO), halo_blocks - 1), 0)),
        w(0), w(nj), cw(0), cw(nj), cb(0), cb(nj),
    ]
    cb3 = conv_b.reshape(conv_b.shape[0], 1, two_f)
    return pl.pallas_call(
        functools.partial(_ffn_front_body, seq_len=group.seq_len),
        grid=(rows // tm, nj),
        in_specs=in_specs,
        out_specs=pl.BlockSpec((tm, tn), lambda i, j: (i, j)),
        out_shape=jax.ShapeDtypeStruct((rows, d_ff), BF16),
        scratch_shapes=[pltpu.VMEM((tm + 2 * HALO, d), BF16),
                        pltpu.VMEM((tm + 2 * HALO, 2 * tn), F32)],
        compiler_params=_params(("arbitrary", "arbitrary")),
        name="ffn_front",
    )(h, h, h, w_up, w_up, conv_w, conv_w, cb3, cb3)


def kernel(x_prompt, x_sample, cache_a_k, cache_a_v, cache_b_k, cache_b_v, c, c_ctx,
           w_ada, b_ada, norm1_g, norm2_g,
           w_in_even, qn_a, kn_a, lam_q1, lam_k1, lam_q2, lam_k2, subln_g, qn_b, kn_b, rpb, w_out_even,
           w_in_odd, vnorm_g, w_sp, b_sp, w_out_odd,
           w_up, conv_w, conv_b, w_down):
    batch, seq, d = x_prompt.shape
    dec_batch, dec_seq, _ = x_sample.shape
    past_len = cache_a_k.shape[2]
    depth = w_ada.shape[0]
    n_even = w_in_even.shape[0]

    groups = (_Group(batch * seq, seq, 0), _Group(dec_batch * dec_seq, dec_seq, 1))
    xs = [x_prompt.reshape(batch * seq, d), x_sample.reshape(dec_batch * dec_seq, d)]

    cond8 = jnp.concatenate([c_ctx[None], c, jnp.zeros((8 - 1 - dec_batch, d), F32)])

    w_in_even_b, w_out_even_b, w_in_odd_b, w_out_odd_b, w_up_b = (
        w_in_even, w_out_even, w_in_odd, w_out_odd, w_up)
    w_down_b = w_down.astype(BF16)

    prompt_qkv = []
    for layer in range(depth):
        mod = _adaln(cond8, w_ada, b_ada, layer).reshape(8, 6, 1, d)
        if layer % 2 == 0:
            e = layer // 2
            lam_init = 0.8 - 0.6 * math.exp(-0.3 * layer)
            lam_vecs = jnp.stack([lam_q1[e], lam_k1[e], lam_q2[e], lam_k2[e]])
            ones = jnp.ones((SECTION,), F32)
            reps = SECTION // HEAD_DIM
            gains = jnp.concatenate([jnp.tile(qn_a[e] * Q_PRESCALE, reps), jnp.tile(kn_a[e], reps), ones,
                                     jnp.tile(qn_b[e] * Q_PRESCALE, reps), jnp.tile(kn_b[e], reps), ones])[None]
            sub_g = subln_g[e][None]
            mixed = []
            for gi, group in enumerate(groups):
                h = _norm_mod(xs[gi], norm1_g, layer, mod, 0, group)
                if gi == 0:
                    qkv = _qkv_proj(h, w_in_even_b, e, gains, group, rope=False, out_dtype=F32)
                    prompt_qkv.append(qkv)
                    oa, ob = _prompt_attn(qkv, lam_vecs, sub_g, lam_init, group.seq_len)
                else:
                    qkv = _qkv_proj(h, w_in_even_b, e, gains, group, rope=True, out_dtype=BF16)
                    ck_a = cache_a_k[:, e].reshape(dec_batch * past_len, SECTION)
                    cv_a = cache_a_v[:, e].reshape(dec_batch * past_len, SECTION)
                    ck_b = cache_b_k[:, e].reshape(dec_batch * past_len, SECTION)
                    cv_b = cache_b_v[:, e].reshape(dec_batch * past_len, SECTION)
                    oa = _latent_diff_attn(qkv, ck_a, cv_a, lam_vecs, sub_g, lam_init,
                                           group.seq_len, past_len)
                    ob = _latent_na_attn(qkv, ck_b, cv_b, rpb[e], group.seq_len, past_len)
                mixed.append(jnp.concatenate([oa, ob], axis=-1))
            for gi, group in enumerate(groups):
                xs[gi] = _matmul_resid(mixed[gi], w_out_even_b, e, xs[gi], mod, 2, group,
                                       name="even_out_proj")
        else:
            o = layer // 2
            for gi, group in enumerate(groups):
                h = _norm_mod(xs[gi], norm1_g, layer, mod, 0, group)
                z = _matmul(h, w_in_odd_b, o, _gelu_epilogue, (), [], F32, name="odd_in_proj")
                gated = _spatial_gate(z, vnorm_g, w_sp, b_sp, o)
                xs[gi] = _matmul_resid(gated, w_out_odd_b, o, xs[gi], mod, 2, group,
                                       name="odd_out_proj")
        for gi, group in enumerate(groups):
            h = _norm_mod(xs[gi], norm2_g, layer, mod, 3, group)
            act = _ffn_front(h, w_up_b, conv_w, conv_b, layer, group)
            half = _matmul(act, w_down_b, layer, _plain_epilogue, (), [], F32,
                           k_parts=2, k_part=0, name="ffn_down_lo")
            xs[gi] = _matmul_resid(act, w_down_b, layer, xs[gi], mod, 5, group, partial_sum=half,
                                   k_parts=2, k_part=1, name="ffn_down_hi")

    def state(col0, shape):
        parts = [q[:, col0:col0 + SECTION].reshape((batch, seq) + shape) for q in prompt_qkv]
        return jnp.stack(parts, axis=1)

    h_a, h_b = N_HEADS_A, N_HEADS_B
    return (xs[0].reshape(batch, seq, d),
            xs[1].reshape(dec_batch, dec_seq, d),
            state(1 * SECTION, (2, h_a, HEAD_DIM)),
            state(2 * SECTION, (h_a, 2 * HEAD_DIM)),
            state(4 * SECTION, (h_b, HEAD_DIM)),
            state(5 * SECTION, (h_b, HEAD_DIM)))
```

```python
import functools
import math

import jax
import jax.numpy as jnp
from jax import lax
from jax.experimental import pallas as pl
from jax.experimental.pallas import tpu as pltpu

F32 = jnp.float32
BF16 = jnp.bfloat16

NORM_EPS = 1e-6
ROPE_BASE = 10000.0
GRID_W = 64
HEAD_DIM = 128
N_HEADS_A = 8
N_HEADS_B = 16
SECTION = 2048
NA_ROWS = 8
NA_COLS = 16
CHUNK = 128
N_GROUPS = 8
MASK_VALUE = -1e30
LOG2E = math.log2(math.e)
Q_PRESCALE = HEAD_DIM ** -0.5 * LOG2E
LANE = 128
VMEM_LIMIT = 56 * 1024 * 1024

_NT = (((1,), (1,)), ((), ()))


def _params(semantics):
    return pltpu.CompilerParams(dimension_semantics=semantics,
                                vmem_limit_bytes=VMEM_LIMIT)


def _rms(x):
    return x * lax.rsqrt(jnp.mean(x * x, axis=-1, keepdims=True) + NORM_EPS)


def _adaln_body(c_ref, w_ref, b_ref, o_ref):
    c = c_ref[...]
    s = c * jax.nn.sigmoid(c)
    o_ref[...] = jnp.dot(s.astype(BF16), w_ref[...].astype(BF16),
                         preferred_element_type=F32) + b_ref[...]


def _adaln(cond8, w_ada, b_ada, layer):
    _, d, n = w_ada.shape
    tn = 1024
    return pl.pallas_call(
        _adaln_body,
        grid=(n // tn,),
        in_specs=[pl.BlockSpec((8, d), lambda j: (0, 0)),
                  pl.BlockSpec((None, d, tn), lambda j: (layer, 0, j)),
                  pl.BlockSpec((None, 1, tn), lambda j: (layer, 0, j))],
        out_specs=pl.BlockSpec((8, tn), lambda j: (0, j)),
        out_shape=jax.ShapeDtypeStruct((8, n), F32),
        compiler_params=_params(("parallel",)),
        name="adaln",
    )(cond8, w_ada, b_ada.reshape(b_ada.shape[0], 1, n))


class _Group:
    def __init__(self, rows, seq_len, cond_base):
        self.rows = rows
        self.seq_len = seq_len
        self.cond_base = cond_base

    def cond(self, row0):
        if self.cond_base == 0:
            return 0
        return self.cond_base + row0 // self.seq_len


def _mod_spec(group, tm, tn, which, col_of):
    return pl.BlockSpec(
        (None, None, 1, tn),
        lambda *ids: (group.cond(ids[0] * tm), which, 0, col_of(*ids)))


def _norm_mod_body(x_ref, g_ref, sh_ref, sc_ref, o_ref):
    y = _rms(x_ref[...]) * g_ref[...]
    o_ref[...] = (y * (1 + sc_ref[...]) + sh_ref[...]).astype(o_ref.dtype)


def _norm_mod(x, gains, layer, mod, shift_idx, group):
    rows, d = x.shape
    tm = 256
    return pl.pallas_call(
        _norm_mod_body,
        grid=(rows // tm,),
        in_specs=[pl.BlockSpec((tm, d), lambda i: (i, 0)),
                  pl.BlockSpec((None, 1, d), lambda i: (layer, 0, 0)),
                  _mod_spec(group, tm, d, shift_idx, lambda i: 0),
                  _mod_spec(group, tm, d, shift_idx + 1, lambda i: 0)],
        out_specs=pl.BlockSpec((tm, d), lambda i: (i, 0)),
        out_shape=jax.ShapeDtypeStruct((rows, d), BF16),
        compiler_params=_params(("parallel",)),
        name="norm_mod",
    )(x, gains.reshape(gains.shape[0], 1, d), mod, mod)


MM_TM = 1024
MM_TN = 512
MM_CHUNK = 256


def _mm_body(*refs, n_extra, epilogue):
    x_ref, w_ref = refs[0], refs[1]
    extras = refs[2:2 + n_extra]
    o_ref = refs[2 + n_extra]
    j = pl.program_id(1)
    w = w_ref[...].astype(BF16)
    for c in range(x_ref.shape[0] // MM_CHUNK):
        rows = slice(c * MM_CHUNK, (c + 1) * MM_CHUNK)
        acc = jnp.dot(x_ref[rows, :], w, preferred_element_type=F32)
        epilogue(acc, extras, o_ref, j, rows)


def _matmul(x, w, layer, epilogue, extras, extra_specs, out_dtype, *,
            k_parts=1, k_part=0, name):
    rows, kdim = x.shape
    n = w.shape[-1]
    tm, tn = MM_TM, MM_TN
    tk = kdim // k_parts
    body = functools.partial(_mm_body, n_extra=len(extras), epilogue=epilogue)
    return pl.pallas_call(
        body,
        grid=(rows // tm, n // tn),
        in_specs=[pl.BlockSpec((tm, tk), lambda i, j: (i, k_part)),
                  pl.BlockSpec((None, tk, tn), lambda i, j: (layer, k_part, j)),
                  *extra_specs],
        out_specs=pl.BlockSpec((tm, tn), lambda i, j: (i, j)),
        out_shape=jax.ShapeDtypeStruct((rows, n), out_dtype),
        compiler_params=_params(("parallel", "arbitrary")),
        name=name,
    )(x, w, *extras)


def _plain_epilogue(acc, extras, o_ref, j, rows):
    o_ref[rows, :] = acc.astype(o_ref.dtype)


def _gelu_epilogue(acc, extras, o_ref, j, rows):
    o_ref[rows, :] = (0.5 * acc * (1 + lax.erf(acc * math.sqrt(0.5)))).astype(o_ref.dtype)


def _resid_epilogue(acc, extras, o_ref, j, rows):
    x_ref, gate_ref = extras[:2]
    if len(extras) == 3:
        acc = extras[2][rows, :] + acc
    o_ref[rows, :] = x_ref[rows, :] + gate_ref[...] * acc


def _matmul_resid(h, w, layer, x, mod, gate_idx, group, *, partial_sum=None,
                  k_parts=1, k_part=0, name):
    tile = pl.BlockSpec((MM_TM, MM_TN), lambda i, j: (i, j))
    specs = [tile, _mod_spec(group, MM_TM, MM_TN, gate_idx, lambda i, j: j)]
    extras = (x, mod)
    if partial_sum is not None:
        specs.append(tile)
        extras += (partial_sum,)
    return _matmul(h, w, layer, _resid_epilogue, extras, specs, F32,
                   k_parts=k_parts, k_part=k_part, name=name)


def _qkv_epilogue(acc, extras, o_ref, j, rows, *, rope):
    g_ref = extras[0]
    sec = (j * MM_TN) // SECTION
    is_v = jnp.logical_or(sec == 2, sec == 5)
    if rope:
        is_rope = sec < 2
        cos = jnp.where(is_rope, extras[1][rows, :], 1.0)
        sin_lo = jnp.where(is_rope, extras[2][rows, :], 0.0)
        sin_hi = jnp.where(is_rope, extras[3][rows, :], 0.0)
    for c in range(MM_TN // LANE):
        cols = slice(c * LANE, (c + 1) * LANE)
        a = acc[:, cols]
        y = jnp.where(is_v, a, _rms(a) * g_ref[:, cols])
        if rope:
            y = (y * cos + pltpu.roll(y, LANE - 32, 1) * sin_lo
                 + pltpu.roll(y, 32, 1) * sin_hi)
        o_ref[rows, cols] = y.astype(o_ref.dtype)


def _rope_tables(n_tok):
    quarter = HEAD_DIM // 4
    t = jnp.arange(n_tok)
    inv = ROPE_BASE ** (-jnp.arange(quarter, dtype=F32) / quarter)

    def ang(p):
        a = p.astype(F32)[:, None] * inv[None, :]
        return jnp.concatenate([a, a], axis=-1)

    angles = jnp.concatenate([ang(t // GRID_W), ang(t % GRID_W)], axis=-1)
    cos, sin = jnp.cos(angles), jnp.sin(angles)
    low = (jnp.arange(HEAD_DIM) % (2 * quarter)) < quarter
    return cos, jnp.where(low, -sin, 0.0), jnp.where(low, 0.0, sin)


def _qkv_proj(h, w, layer, gains, group, *, rope, out_dtype):
    extras = [gains]
    specs = [pl.BlockSpec((1, MM_TN), lambda i, j: (0, j))]
    if rope:
        tiles_per_seq = group.seq_len // MM_TM
        tab = pl.BlockSpec((MM_TM, HEAD_DIM), lambda i, j: (i % tiles_per_seq, 0))
        extras += list(_rope_tables(group.seq_len))
        specs += [tab, tab, tab]
    epi = functools.partial(_qkv_epilogue, rope=rope)
    return _matmul(h, w, layer, epi, tuple(extras), specs, out_dtype, name="qkv_proj")


def _lam_value(lam_ref, lam_init):
    v = lam_ref[...]
    t1 = jnp.sum(v[0:1] * v[1:2], axis=-1, keepdims=True)
    t2 = jnp.sum(v[2:3] * v[3:4], axis=-1, keepdims=True)
    return jnp.exp(t1) - jnp.exp(t2) + lam_init


def _scores(q, k):
    return lax.dot_general(q.astype(BF16), k.astype(BF16), _NT,
                           preferred_element_type=F32)


def _softmax(s):
    e = jnp.exp2(s - jnp.max(s, axis=-1, keepdims=True))
    return e / jnp.sum(e, axis=-1, keepdims=True)


def _joint_exp(s_a, s_b):
    m = jnp.maximum(jnp.max(s_a, axis=-1, keepdims=True),
                    jnp.max(s_b, axis=-1, keepdims=True))
    e_a = jnp.exp2(s_a - m)
    e_b = jnp.exp2(s_b - m)
    denom = (jnp.sum(e_a, axis=-1, keepdims=True)
             + jnp.sum(e_b, axis=-1, keepdims=True))
    return e_a.astype(BF16), e_b.astype(BF16), denom


def _subln(o, g_ref, lam_init):
    return _rms(o) * g_ref[...] * (1.0 - lam_init)


def _prompt_attn_body(lam_ref, q0, q1, k0, k1, va, qb, kb, vb, g_ref,
                      oa_ref, ob_ref, *, lam_init):
    lam = _lam_value(lam_ref, lam_init)
    a = _softmax(_scores(q0[...], k0[...])) - lam * _softmax(_scores(q1[...], k1[...]))
    o = jnp.dot(a.astype(BF16), va[...].astype(BF16), preferred_element_type=F32)
    oa_ref[...] = _subln(o, g_ref, lam_init).astype(oa_ref.dtype)
    for t in range(2):
        cols = slice(t * HEAD_DIM, (t + 1) * HEAD_DIM)
        p = _softmax(_scores(qb[:, cols], kb[:, cols]))
        ob_ref[:, cols] = jnp.dot(p.astype(BF16), vb[:, cols].astype(BF16),
                                  preferred_element_type=F32).astype(ob_ref.dtype)


def _prompt_attn(qkv, lam_vecs, subln_g, lam_init, seq_len):
    rows = qkv.shape[0]
    nb = rows // seq_len
    sec128 = SECTION // HEAD_DIM
    sec256 = SECTION // (2 * HEAD_DIM)

    def blk(width, col0):
        return pl.BlockSpec((seq_len, width), lambda b, h: (b, col0 + h))

    half = sec128 // 2
    in_specs = [
        pl.BlockSpec((4, HEAD_DIM), lambda b, h: (0, 0)),
        blk(HEAD_DIM, 0), blk(HEAD_DIM, half),
        blk(HEAD_DIM, sec128), blk(HEAD_DIM, sec128 + half),
        blk(2 * HEAD_DIM, 2 * sec256),
        blk(2 * HEAD_DIM, 3 * sec256),
        blk(2 * HEAD_DIM, 4 * sec256),
        blk(2 * HEAD_DIM, 5 * sec256),
        pl.BlockSpec((1, 2 * HEAD_DIM), lambda b, h: (0, 0)),
    ]
    out_spec = pl.BlockSpec((seq_len, 2 * HEAD_DIM), lambda b, h: (b, h))
    out = jax.ShapeDtypeStruct((rows, SECTION), BF16)
    return pl.pallas_call(
        functools.partial(_prompt_attn_body, lam_init=lam_init),
        grid=(nb, N_HEADS_A),
        in_specs=in_specs,
        out_specs=[out_spec, out_spec],
        out_shape=[out, out],
        compiler_params=_params(("parallel", "parallel")),
        name="prompt_attn",
    )(lam_vecs, qkv, qkv, qkv, qkv, qkv, qkv, qkv, qkv, subln_g)


DIFF_KEY_CHUNK = 1024


def _latent_diff_body(lam_ref, q0, q1, k0, k1, v, kc0, kc1, vc, g_ref, o_ref,
                      *, lam_init):
    lam = _lam_value(lam_ref, lam_init)
    v_ctx = vc[...].astype(BF16)
    n_lat = k0.shape[0]

    def attend(q, k, kc):
        qv = q[...]
        s = _scores(qv, kc[...])
        m = jnp.max(s, axis=-1, keepdims=True)
        e = jnp.exp2(s - m)
        denom = jnp.sum(e, axis=-1, keepdims=True)
        acc = jnp.dot(e.astype(BF16), v_ctx, preferred_element_type=F32)
        for c in range(n_lat // DIFF_KEY_CHUNK):
            keys = slice(c * DIFF_KEY_CHUNK, (c + 1) * DIFF_KEY_CHUNK)
            s = _scores(qv, k[keys, :])
            m_new = jnp.maximum(m, jnp.max(s, axis=-1, keepdims=True))
            alpha = jnp.exp2(m - m_new)
            e = jnp.exp2(s - m_new)
            denom = denom * alpha + jnp.sum(e, axis=-1, keepdims=True)
            acc = acc * alpha + jnp.dot(e.astype(BF16), v[keys, :], preferred_element_type=F32)
            m = m_new
        return acc / denom

    o = attend(q0, k0, kc0) - lam * attend(q1, k1, kc1)
    o_ref[...] = _subln(o, g_ref, lam_init).astype(o_ref.dtype)


def _latent_diff_attn(qkv, cache_k, cache_v, lam_vecs, subln_g, lam_init,
                      seq_len, ctx_len):
    rows = qkv.shape[0]
    nb = rows // seq_len
    tq = 256
    nq = seq_len // tq
    sec128 = SECTION // HEAD_DIM
    sec256 = SECTION // (2 * HEAD_DIM)
    half = sec128 // 2

    def qblk(col0):
        return pl.BlockSpec((tq, HEAD_DIM), lambda b, h, i: (b * nq + i, col0 + h))

    def kvblk(n, width, col0):
        return pl.BlockSpec((n, width), lambda b, h, i: (b, col0 + h))

    in_specs = [
        pl.BlockSpec((4, HEAD_DIM), lambda b, h, i: (0, 0)),
        qblk(0), qblk(half),
        kvblk(seq_len, HEAD_DIM, sec128), kvblk(seq_len, HEAD_DIM, sec128 + half),
        kvblk(seq_len, 2 * HEAD_DIM, 2 * sec256),
        kvblk(ctx_len, HEAD_DIM, 0), kvblk(ctx_len, HEAD_DIM, half),
        kvblk(ctx_len, 2 * HEAD_DIM, 0),
        pl.BlockSpec((1, 2 * HEAD_DIM), lambda b, h, i: (0, 0)),
    ]
    return pl.pallas_call(
        functools.partial(_latent_diff_body, lam_init=lam_init),
        grid=(nb, N_HEADS_A, nq),
        in_specs=in_specs,
        out_specs=pl.BlockSpec((tq, 2 * HEAD_DIM), lambda b, h, i: (b * nq + i, h)),
        out_shape=jax.ShapeDtypeStruct((rows, SECTION), BF16),
        compiler_params=_params(("parallel", "parallel", "arbitrary")),
        name="latent_diff_attn",
    )(lam_vecs, qkv, qkv, qkv, qkv, qkv, cache_k, cache_k, cache_v, subln_g)


NA_Q_ROWS = 8
NA_K_ROWS = 16


def _na_band_start(i, grid_rows):
    return jnp.clip(i * NA_Q_ROWS - NA_ROWS // 2, 0, grid_rows - NA_K_ROWS)


NA_D_MIN = -NA_Q_ROWS
NA_D_MAX = NA_K_ROWS + NA_ROWS - 2


def _na_pair_tables(rpb):
    n_h, n_dr, n_dc = rpb.shape
    u = jnp.concatenate([rpb[..., NA_COLS - 1:],
                         jnp.zeros((n_h, n_dr, LANE - n_dc), F32),
                         rpb[..., :NA_COLS - 1]], axis=-1)
    skew = jnp.tile(u, (1, 1, GRID_W))[..., :GRID_W * (LANE - 1)]
    toep = skew.reshape(n_h, n_dr, GRID_W, LANE - 1)[..., :GRID_W]
    cols = jnp.arange(GRID_W)
    cstart = jnp.clip(cols - NA_COLS // 2, 0, GRID_W - NA_COLS)
    col_ok = (cols[None, :] >= cstart[:, None]) & (cols[None, :] < cstart[:, None] + NA_COLS)
    toep = jnp.where(col_ok, toep, MASK_VALUE)
    ext = jnp.pad(toep, ((0, 0), (-NA_D_MIN, NA_D_MAX + 1 - n_dr), (0, 0), (0, 0)),
                  constant_values=MASK_VALUE)
    return jnp.concatenate([ext[:, :-1], ext[:, 1:]], axis=-1)


def _latent_na_body(q, k, v, kc, vc, tab_ref, o_ref, bias_ref, *, grid_rows):
    i = pl.program_id(2)
    n_blocks = grid_rows // NA_Q_ROWS
    row0 = i * NA_Q_ROWS
    band0 = _na_band_start(i, grid_rows)

    @pl.when(jnp.logical_or(i <= 1, i == n_blocks - 1))
    def _():
        lane = lax.broadcasted_iota(jnp.int32, (GRID_W, LANE), 1)
        for rq in range(NA_Q_ROWS):
            r = row0 + rq
            rs = jnp.clip(r - NA_ROWS // 2, 0, grid_rows - NA_ROWS)

            def penalty(kr):
                ok = jnp.logical_and(kr >= rs, kr < rs + NA_ROWS)
                return jnp.where(ok, 0.0, MASK_VALUE)

            for p in range(NA_K_ROWS // 2):
                kr = band0 + 2 * p
                d = kr - r + NA_ROWS - 1
                pen = jnp.where(lane < GRID_W, penalty(kr), penalty(kr + 1))
                bias_ref[rq * GRID_W:(rq + 1) * GRID_W, p * LANE:(p + 1) * LANE] = (
                    tab_ref[d - NA_D_MIN] * LOG2E + pen)

    start = pl.multiple_of(band0 * GRID_W, GRID_W)
    band = pl.ds(start, NA_K_ROWS * GRID_W)
    k_band, v_band = k[band, :], v[band, :]
    k_ctx, v_ctx = kc[...], vc[...].astype(BF16)
    half = q.shape[0] // 2
    for part in range(2):
        rows = slice(part * half, (part + 1) * half)
        e_loc, e_ctx, denom = _joint_exp(_scores(q[rows, :], k_band) + bias_ref[rows, :],
                                         _scores(q[rows, :], k_ctx))
        o = (jnp.dot(e_loc, v_band, preferred_element_type=F32)
             + jnp.dot(e_ctx, v_ctx, preferred_element_type=F32))
        o_ref[rows, :] = (o / denom).astype(o_ref.dtype)


def _latent_na_attn(qkv, cache_k, cache_v, rpb, seq_len, ctx_len):
    rows = qkv.shape[0]
    nb = rows // seq_len
    grid_rows = seq_len // GRID_W
    n_blocks = grid_rows // NA_Q_ROWS
    tq = NA_Q_ROWS * GRID_W
    sec128 = SECTION // HEAD_DIM
    tables = _na_pair_tables(rpb)

    def kvblk(n, col0):
        return pl.BlockSpec((n, HEAD_DIM), lambda h, b, i: (b, col0 + h))

    in_specs = [
        pl.BlockSpec((tq, HEAD_DIM), lambda h, b, i: (b * n_blocks + i, 3 * sec128 + h)),
        kvblk(seq_len, 4 * sec128), kvblk(seq_len, 5 * sec128),
        kvblk(ctx_len, 0), kvblk(ctx_len, 0),
        pl.BlockSpec((None,) + tables.shape[1:], lambda h, b, i: (h, 0, 0, 0)),
    ]
    return pl.pallas_call(
        functools.partial(_latent_na_body, grid_rows=grid_rows),
        grid=(N_HEADS_B, nb, n_blocks),
        in_specs=in_specs,
        out_specs=pl.BlockSpec((tq, HEAD_DIM), lambda h, b, i: (b * n_blocks + i, h)),
        out_shape=jax.ShapeDtypeStruct((rows, SECTION), BF16),
        scratch_shapes=[pltpu.VMEM((tq, NA_K_ROWS * GRID_W), F32)],
        compiler_params=_params(("arbitrary", "arbitrary", "arbitrary")),
        name="latent_na_attn",
    )(qkv, qkv, qkv, cache_k, cache_v, tables)


def _spatial_gate_body(u_ref, v_ref, g_ref, ws_ref, bs_ref, o_ref):
    vn = (_rms(v_ref[...]) * g_ref[...]).astype(BF16)
    width = vn.shape[1] // N_GROUPS
    for g in range(N_GROUPS):
        cols = slice(g * width, (g + 1) * width)
        sv = jnp.dot(ws_ref[g].astype(BF16), vn[:, cols],
                     preferred_element_type=F32) + bs_ref[:, g:g + 1]
        o_ref[:, cols] = (u_ref[:, cols] * sv).astype(o_ref.dtype)


def _spatial_gate(z, vnorm_g, w_sp, b_sp, layer):
    rows, two_d = z.shape
    d = two_d // 2
    return pl.pallas_call(
        _spatial_gate_body,
        grid=(rows // CHUNK,),
        in_specs=[pl.BlockSpec((CHUNK, d), lambda n: (n, 0)),
                  pl.BlockSpec((CHUNK, d), lambda n: (n, 1)),
                  pl.BlockSpec((None, 1, d), lambda n: (layer, 0, 0)),
                  pl.BlockSpec((None, N_GROUPS, CHUNK, CHUNK), lambda n: (layer, 0, 0, 0)),
                  pl.BlockSpec((None, CHUNK, N_GROUPS), lambda n: (layer, 0, 0))],
        out_specs=pl.BlockSpec((CHUNK, d), lambda n: (n, 0)),
        out_shape=jax.ShapeDtypeStruct((rows, d), BF16),
        compiler_params=_params(("parallel",)),
        name="spatial_gate",
    )(z, z, vnorm_g.reshape(vnorm_g.shape[0], 1, d), w_sp, jnp.swapaxes(b_sp, 1, 2))


FFN_TN = 256
HALO = 32
SUB = 8
FFN_CHUNKS = (256, 256, 256, 256)


def _ffn_front_body(x_ref, xp_ref, xn_ref, wg_ref, wu_ref, cwg_ref, cwu_ref,
                    cbg_ref, cbu_ref, o_ref, xs_ref, acc_ref, *, seq_len):
    tm = x_ref.shape[0]
    i = pl.program_id(0)

    @pl.when(pl.program_id(1) == 0)
    def _():
        xs_ref[0:HALO, :] = xp_ref[...]
        xs_ref[HALO:HALO + tm, :] = x_ref[...]
        xs_ref[HALO + tm:, :] = xn_ref[...]

    starts = [sum(FFN_CHUNKS[:c]) for c in range(len(FFN_CHUNKS) + 1)]
    n_chunks = len(FFN_CHUNKS)

    tn = wg_ref.shape[1]
    w = jnp.concatenate([wg_ref[...].astype(BF16), wu_ref[...].astype(BF16)], axis=1)

    def project(c):
        lo = 0 if c == 0 else HALO + starts[c]
        hi = tm + 2 * HALO if c == n_chunks - 1 else HALO + starts[c + 1]
        acc_ref[lo:hi, :] = jnp.dot(xs_ref[lo:hi, :], w, preferred_element_type=F32)

    def conv(col0, cw_ref, cb_ref, c, at_start, at_end):
        n = FFN_CHUNKS[c]
        r0 = HALO + starts[c]
        cols = slice(col0, col0 + tn)
        x_prev = acc_ref[r0 - 1:r0 - 1 + n, cols]
        x_next = acc_ref[r0 + 1:r0 + 1 + n, cols]
        row = lax.broadcasted_iota(jnp.int32, (SUB, 1), 0)
        head = jnp.where(jnp.logical_and(at_start, row == 0), 0.0, x_prev[:SUB])
        tail = jnp.where(jnp.logical_and(at_end, row == SUB - 1), 0.0, x_next[n - SUB:])
        x_prev = jnp.concatenate([head, x_prev[SUB:]], axis=0)
        x_next = jnp.concatenate([x_next[:n - SUB], tail], axis=0)
        return (cw_ref[0:1, :] * x_prev + cw_ref[1:2, :] * acc_ref[r0:r0 + n, cols]
                + cw_ref[2:3, :] * x_next + cb_ref[...])

    def gate(c):
        first = i * tm + starts[c]
        at_start = first % seq_len == 0
        at_end = (first + FFN_CHUNKS[c]) % seq_len == 0
        g = conv(0, cwg_ref, cbg_ref, c, at_start, at_end)
        u = conv(tn, cwu_ref, cbu_ref, c, at_start, at_end)
        o_ref[starts[c]:starts[c + 1], :] = (g * jax.nn.sigmoid(g) * u).astype(o_ref.dtype)

    project(0)
    for c in range(n_chunks):
        if c + 1 < n_chunks:
            project(c + 1)
        gate(c)


def _ffn_front(h, w_up, conv_w, conv_b, layer, group):
    rows, d = h.shape
    two_f = w_up.shape[-1]
    d_ff = two_f // 2
    tm, tn = MM_TM, FFN_TN
    nj = d_ff // tn
    starts = {sum(FFN_CHUNKS[:c]) for c in range(len(FFN_CHUNKS))}
    assert sum(FFN_CHUNKS) == tm and d_ff % tn == 0
    assert tm % group.seq_len == 0 or group.seq_len % tm == 0
    assert all(s in starts for s in range(0, tm, group.seq_len))
    halo_blocks = rows // HALO

    def w(off):
        return pl.BlockSpec((None, d, tn), lambda i, j: (layer, 0, off + j))

    def cw(off):
        return pl.BlockSpec((None, 3, tn), lambda i, j: (layer, 0, off + j))

    def cb(off):
        return pl.BlockSpec((None, 1, tn), lambda i, j: (layer, 0, off + j))

    in_specs = [
        pl.BlockSpec((tm, d), lambda i, j: (i, 0)),
        pl.BlockSpec((HALO, d), lambda i, j: (jnp.maximum(i * (tm // HALO) - 1, 0), 0)),
        pl.BlockSpec((HALO, d), lambda i, j: (jnp.minimum((i + 1) * (tm // HALO), halo_blocks - 1), 0)),
        w(0), w(nj), cw(0), cw(nj), cb(0), cb(nj),
    ]
    cb3 = conv_b.reshape(conv_b.shape[0], 1, two_f)
    return pl.pallas_call(
        functools.partial(_ffn_front_body, seq_len=group.seq_len),
        grid=(rows // tm, nj),
        in_specs=in_specs,
        out_specs=pl.BlockSpec((tm, tn), lambda i, j: (i, j)),
        out_shape=jax.ShapeDtypeStruct((rows, d_ff), BF16),
        scratch_shapes=[pltpu.VMEM((tm + 2 * HALO, d), BF16),
                        pltpu.VMEM((tm + 2 * HALO, 2 * tn), F32)],
        compiler_params=_params(("arbitrary", "arbitrary")),
        name="ffn_front",
    )(h, h, h, w_up, w_up, conv_w, conv_w, cb3, cb3)


def kernel(x_prompt, x_sample, cache_a_k, cache_a_v, cache_b_k, cache_b_v, c, c_ctx,
           w_ada, b_ada, norm1_g, norm2_g,
           w_in_even, qn_a, kn_a, lam_q1, lam_k1, lam_q2, lam_k2, subln_g, qn_b, kn_b, rpb, w_out_even,
           w_in_odd, vnorm_g, w_sp, b_sp, w_out_odd,
           w_up, conv_w, conv_b, w_down):
    batch, seq, d = x_prompt.shape
    dec_batch, dec_seq, _ = x_sample.shape
    past_len = cache_a_k.shape[2]
    depth = w_ada.shape[0]
    n_even = w_in_even.shape[0]

    groups = (_Group(batch * seq, seq, 0), _Group(dec_batch * dec_seq, dec_seq, 1))
    xs = [x_prompt.reshape(batch * seq, d), x_sample.reshape(dec_batch * dec_seq, d)]

    cond8 = jnp.concatenate([c_ctx[None], c, jnp.zeros((8 - 1 - dec_batch, d), F32)])

    w_in_even_b, w_out_even_b, w_in_odd_b, w_out_odd_b, w_up_b = (
        w_in_even, w_out_even, w_in_odd, w_out_odd, w_up)
    w_down_b = w_down.astype(BF16)

    prompt_qkv = []
    for layer in range(depth):
        mod = _adaln(cond8, w_ada, b_ada, layer).reshape(8, 6, 1, d)
        if layer % 2 == 0:
            e = layer // 2
            lam_init = 0.8 - 0.6 * math.exp(-0.3 * layer)
            lam_vecs = jnp.stack([lam_q1[e], lam_k1[e], lam_q2[e], lam_k2[e]])
            ones = jnp.ones((SECTION,), F32)
            reps = SECTION // HEAD_DIM
            gains = jnp.concatenate([jnp.tile(qn_a[e] * Q_PRESCALE, reps), jnp.tile(kn_a[e], reps), ones,
                                     jnp.tile(qn_b[e] * Q_PRESCALE, reps), jnp.tile(kn_b[e], reps), ones])[None]
            sub_g = subln_g[e][None]
            mixed = []
            for gi, group in enumerate(groups):
                h = _norm_mod(xs[gi], norm1_g, layer, mod, 0, group)
                if gi == 0:
                    qkv = _qkv_proj(h, w_in_even_b, e, gains, group, rope=False, out_dtype=F32)
                    prompt_qkv.append(qkv)
                    oa, ob = _prompt_attn(qkv, lam_vecs, sub_g, lam_init, group.seq_len)
                else:
                    qkv = _qkv_proj(h, w_in_even_b, e, gains, group, rope=True, out_dtype=BF16)
                    ck_a = cache_a_k[:, e].reshape(dec_batch * past_len, SECTION)
                    cv_a = cache_a_v[:, e].reshape(dec_batch * past_len, SECTION)
                    ck_b = cache_b_k[:, e].reshape(dec_batch * past_len, SECTION)
                    cv_b = cache_b_v[:, e].reshape(dec_batch * past_len, SECTION)
                    oa = _latent_diff_attn(qkv, ck_a, cv_a, lam_vecs, sub_g, lam_init,
                                           group.seq_len, past_len)
                    ob = _latent_na_attn(qkv, ck_b, cv_b, rpb[e], group.seq_len, past_len)
                mixed.append(jnp.concatenate([oa, ob], axis=-1))
            for gi, group in enumerate(groups):
                xs[gi] = _matmul_resid(mixed[gi], w_out_even_b, e, xs[gi], mod, 2, group,
                                       name="even_out_proj")
        else:
            o = layer // 2
            for gi, group in enumerate(groups):
                h = _norm_mod(xs[gi], norm1_g, layer, mod, 0, group)
                z = _matmul(h, w_in_odd_b, o, _gelu_epilogue, (), [], F32, name="odd_in_proj")
                gated = _spatial_gate(z, vnorm_g, w_sp, b_sp, o)
                xs[gi] = _matmul_resid(gated, w_out_odd_b, o, xs[gi], mod, 2, group,
                                       name="odd_out_proj")
        for gi, group in enumerate(groups):
            h = _norm_mod(xs[gi], norm2_g, layer, mod, 3, group)
            act = _ffn_front(h, w_up_b, conv_w, conv_b, layer, group)
            half = _matmul(act, w_down_b, layer, _plain_epilogue, (), [], F32,
                           k_parts=2, k_part=0, name="ffn_down_lo")
            xs[gi] = _matmul_resid(act, w_down_b, layer, xs[gi], mod, 5, group, partial_sum=half,
                                   k_parts=2, k_part=1, name="ffn_down_hi")

    def state(col0, shape):
        parts = [q[:, col0:col0 + SECTION].reshape((batch, seq) + shape) for q in prompt_qkv]
        return jnp.stack(parts, axis=1)

    h_a, h_b = N_HEADS_A, N_HEADS_B
    return (xs[0].reshape(batch, seq, d),
            xs[1].reshape(dec_batch, dec_seq, d),
            state(1 * SECTION, (2, h_a, HEAD_DIM)),
            state(2 * SECTION, (h_a, 2 * HEAD_DIM)),
            state(4 * SECTION, (h_b, HEAD_DIM)),
            state(5 * SECTION, (h_b, HEAD_DIM)))
```

```python
import functools
import math

import jax
import jax.numpy as jnp
from jax import lax
from jax.experimental import pallas as pl
from jax.experimental.pallas import tpu as pltpu

F32 = jnp.float32
BF16 = jnp.bfloat16

NORM_EPS = 1e-6
ROPE_BASE = 10000.0
GRID_W = 64
HEAD_DIM = 128
N_HEADS_A = 8
N_HEADS_B = 16
SECTION = 2048
NA_ROWS = 8
NA_COLS = 16
CHUNK = 128
N_GROUPS = 8
MASK_VALUE = -1e30
LOG2E = math.log2(math.e)
Q_PRESCALE = HEAD_DIM ** -0.5 * LOG2E
LANE = 128
VMEM_LIMIT = 56 * 1024 * 1024

_NT = (((1,), (1,)), ((), ()))


def _params(semantics):
    return pltpu.CompilerParams(dimension_semantics=semantics,
                                vmem_limit_bytes=VMEM_LIMIT)


def _rms(x):
    return x * lax.rsqrt(jnp.mean(x * x, axis=-1, keepdims=True) + NORM_EPS)


def _adaln_body(c_ref, w_ref, b_ref, o_ref):
    c = c_ref[...]
    s = c * jax.nn.sigmoid(c)
    o_ref[...] = jnp.dot(s.astype(BF16), w_ref[...].astype(BF16),
                         preferred_element_type=F32) + b_ref[...]


def _adaln(cond8, w_ada, b_ada, layer):
    _, d, n = w_ada.shape
    tn = 1024
    return pl.pallas_call(
        _adaln_body,
        grid=(n // tn,),
        in_specs=[pl.BlockSpec((8, d), lambda j: (0, 0)),
                  pl.BlockSpec((None, d, tn), lambda j: (layer, 0, j)),
                  pl.BlockSpec((None, 1, tn), lambda j: (layer, 0, j))],
        out_specs=pl.BlockSpec((8, tn), lambda j: (0, j)),
        out_shape=jax.ShapeDtypeStruct((8, n), F32),
        compiler_params=_params(("parallel",)),
        name="adaln",
    )(cond8, w_ada, b_ada.reshape(b_ada.shape[0], 1, n))


class _Group:
    def __init__(self, rows, seq_len, cond_base):
        self.rows = rows
        self.seq_len = seq_len
        self.cond_base = cond_base

    def cond(self, row0):
        if self.cond_base == 0:
            return 0
        return self.cond_base + row0 // self.seq_len


def _mod_spec(group, tm, tn, which, col_of):
    return pl.BlockSpec(
        (None, None, 1, tn),
        lambda *ids: (group.cond(ids[0] * tm), which, 0, col_of(*ids)))


def _norm_mod_body(x_ref, g_ref, sh_ref, sc_ref, o_ref):
    y = _rms(x_ref[...]) * g_ref[...]
    o_ref[...] = (y * (1 + sc_ref[...]) + sh_ref[...]).astype(o_ref.dtype)


def _norm_mod(x, gains, layer, mod, shift_idx, group):
    rows, d = x.shape
    tm = 256
    return pl.pallas_call(
        _norm_mod_body,
        grid=(rows // tm,),
        in_specs=[pl.BlockSpec((tm, d), lambda i: (i, 0)),
                  pl.BlockSpec((None, 1, d), lambda i: (layer, 0, 0)),
                  _mod_spec(group, tm, d, shift_idx, lambda i: 0),
                  _mod_spec(group, tm, d, shift_idx + 1, lambda i: 0)],
        out_specs=pl.BlockSpec((tm, d), lambda i: (i, 0)),
        out_shape=jax.ShapeDtypeStruct((rows, d), BF16),
        compiler_params=_params(("parallel",)),
        name="norm_mod",
    )(x, gains.reshape(gains.shape[0], 1, d), mod, mod)


MM_TM = 1024
MM_TN = 512
MM_CHUNK = 256


def _mm_body(*refs, n_x, n_extra, epilogue):
    x_refs, w_ref = refs[:n_x], refs[n_x]
    extras = refs[n_x + 1:n_x + 1 + n_extra]
    o_ref = refs[n_x + 1 + n_extra]
    j = pl.program_id(1)
    w = w_ref[...].astype(BF16)
    for c in range(x_refs[0].shape[0] // MM_CHUNK):
        rows = slice(c * MM_CHUNK, (c + 1) * MM_CHUNK)
        acc, k0 = None, 0
        for x_ref in x_refs:
            k1 = k0 + x_ref.shape[1]
            part = jnp.dot(x_ref[rows, :], w[k0:k1, :], preferred_element_type=F32)
            acc = part if acc is None else acc + part
            k0 = k1
        epilogue(acc, extras, o_ref, j, rows)


def _matmul(x, w, layer, epilogue, extras, extra_specs, out_dtype, *,
            k_parts=1, k_part=0, name):
    xs = x if isinstance(x, tuple) else (x,)
    assert len(xs) == 1 or k_parts == 1
    rows = xs[0].shape[0]
    kdim = sum(p.shape[1] for p in xs)
    n = w.shape[-1]
    tm, tn = MM_TM, MM_TN
    tk = kdim // k_parts
    body = functools.partial(_mm_body, n_x=len(xs), n_extra=len(extras), epilogue=epilogue)
    x_specs = [pl.BlockSpec((tm, p.shape[1] // k_parts), lambda i, j: (i, k_part)) for p in xs]
    return pl.pallas_call(
        body,
        grid=(rows // tm, n // tn),
        in_specs=[*x_specs,
                  pl.BlockSpec((None, tk, tn), lambda i, j: (layer, k_part, j)),
                  *extra_specs],
        out_specs=pl.BlockSpec((tm, tn), lambda i, j: (i, j)),
        out_shape=jax.ShapeDtypeStruct((rows, n), out_dtype),
        compiler_params=_params(("parallel", "arbitrary")),
        name=name,
    )(*xs, w, *extras)


def _plain_epilogue(acc, extras, o_ref, j, rows):
    o_ref[rows, :] = acc.astype(o_ref.dtype)


def _gelu_epilogue(acc, extras, o_ref, j, rows):
    o_ref[rows, :] = (0.5 * acc * (1 + lax.erf(acc * math.sqrt(0.5)))).astype(o_ref.dtype)


def _resid_epilogue(acc, extras, o_ref, j, rows):
    x_ref, gate_ref = extras[:2]
    if len(extras) == 3:
        acc = extras[2][rows, :] + acc
    o_ref[rows, :] = x_ref[rows, :] + gate_ref[...] * acc


def _matmul_resid(h, w, layer, x, mod, gate_idx, group, *, partial_sum=None,
                  k_parts=1, k_part=0, name):
    tile = pl.BlockSpec((MM_TM, MM_TN), lambda i, j: (i, j))
    specs = [tile, _mod_spec(group, MM_TM, MM_TN, gate_idx, lambda i, j: j)]
    extras = (x, mod)
    if partial_sum is not None:
        specs.append(tile)
        extras += (partial_sum,)
    return _matmul(h, w, layer, _resid_epilogue, extras, specs, F32,
                   k_parts=k_parts, k_part=k_part, name=name)


def _qkv_epilogue(acc, extras, o_ref, j, rows, *, rope):
    g_ref = extras[0]
    sec = (j * MM_TN) // SECTION
    is_v = jnp.logical_or(sec == 2, sec == 5)
    if rope:
        is_rope = sec < 2
        cos = jnp.where(is_rope, extras[1][rows, :], 1.0)
        sin_lo = jnp.where(is_rope, extras[2][rows, :], 0.0)
        sin_hi = jnp.where(is_rope, extras[3][rows, :], 0.0)
    for c in range(MM_TN // LANE):
        cols = slice(c * LANE, (c + 1) * LANE)
        a = acc[:, cols]
        y = jnp.where(is_v, a, _rms(a) * g_ref[:, cols])
        if rope:
            y = (y * cos + pltpu.roll(y, LANE - 32, 1) * sin_lo
                 + pltpu.roll(y, 32, 1) * sin_hi)
        o_ref[rows, cols] = y.astype(o_ref.dtype)


def _rope_tables(n_tok):
    quarter = HEAD_DIM // 4
    t = jnp.arange(n_tok)
    inv = ROPE_BASE ** (-jnp.arange(quarter, dtype=F32) / quarter)

    def ang(p):
        a = p.astype(F32)[:, None] * inv[None, :]
        return jnp.concatenate([a, a], axis=-1)

    angles = jnp.concatenate([ang(t // GRID_W), ang(t % GRID_W)], axis=-1)
    cos, sin = jnp.cos(angles), jnp.sin(angles)
    low = (jnp.arange(HEAD_DIM) % (2 * quarter)) < quarter
    return cos, jnp.where(low, -sin, 0.0), jnp.where(low, 0.0, sin)


def _qkv_proj(h, w, layer, gains, group, *, rope, out_dtype):
    extras = [gains]
    specs = [pl.BlockSpec((1, MM_TN), lambda i, j: (0, j))]
    if rope:
        tiles_per_seq = group.seq_len // MM_TM
        tab = pl.BlockSpec((MM_TM, HEAD_DIM), lambda i, j: (i % tiles_per_seq, 0))
        extras += list(_rope_tables(group.seq_len))
        specs += [tab, tab, tab]
    epi = functools.partial(_qkv_epilogue, rope=rope)
    return _matmul(h, w, layer, epi, tuple(extras), specs, out_dtype, name="qkv_proj")


def _lam_value(lam_ref, lam_init):
    v = lam_ref[...]
    t1 = jnp.sum(v[0:1] * v[1:2], axis=-1, keepdims=True)
    t2 = jnp.sum(v[2:3] * v[3:4], axis=-1, keepdims=True)
    return jnp.exp(t1) - jnp.exp(t2) + lam_init


def _scores(q, k):
    return lax.dot_general(q.astype(BF16), k.astype(BF16), _NT,
                           preferred_element_type=F32)


def _softmax(s):
    e = jnp.exp2(s - jnp.max(s, axis=-1, keepdims=True))
    return e / jnp.sum(e, axis=-1, keepdims=True)


def _joint_exp(s_a, s_b):
    m = jnp.maximum(jnp.max(s_a, axis=-1, keepdims=True),
                    jnp.max(s_b, axis=-1, keepdims=True))
    e_a = jnp.exp2(s_a - m)
    e_b = jnp.exp2(s_b - m)
    denom = (jnp.sum(e_a, axis=-1, keepdims=True)
             + jnp.sum(e_b, axis=-1, keepdims=True))
    return e_a.astype(BF16), e_b.astype(BF16), denom


def _subln(o, g_ref, lam_init):
    return _rms(o) * g_ref[...] * (1.0 - lam_init)


def _prompt_attn_body(lam_ref, q0, q1, k0, k1, va, qb, kb, vb, g_ref,
                      oa_ref, ob_ref, *, lam_init):
    lam = _lam_value(lam_ref, lam_init)
    a = _softmax(_scores(q0[...], k0[...])) - lam * _softmax(_scores(q1[...], k1[...]))
    o = jnp.dot(a.astype(BF16), va[...].astype(BF16), preferred_element_type=F32)
    oa_ref[...] = _subln(o, g_ref, lam_init).astype(oa_ref.dtype)
    for t in range(2):
        cols = slice(t * HEAD_DIM, (t + 1) * HEAD_DIM)
        p = _softmax(_scores(qb[:, cols], kb[:, cols]))
        ob_ref[:, cols] = jnp.dot(p.astype(BF16), vb[:, cols].astype(BF16),
                                  preferred_element_type=F32).astype(ob_ref.dtype)


def _prompt_attn(qkv, lam_vecs, subln_g, lam_init, seq_len):
    rows = qkv.shape[0]
    nb = rows // seq_len
    sec128 = SECTION // HEAD_DIM
    sec256 = SECTION // (2 * HEAD_DIM)

    def blk(width, col0):
        return pl.BlockSpec((seq_len, width), lambda b, h: (b, col0 + h))

    half = sec128 // 2
    in_specs = [
        pl.BlockSpec((4, HEAD_DIM), lambda b, h: (0, 0)),
        blk(HEAD_DIM, 0), blk(HEAD_DIM, half),
        blk(HEAD_DIM, sec128), blk(HEAD_DIM, sec128 + half),
        blk(2 * HEAD_DIM, 2 * sec256),
        blk(2 * HEAD_DIM, 3 * sec256),
        blk(2 * HEAD_DIM, 4 * sec256),
        blk(2 * HEAD_DIM, 5 * sec256),
        pl.BlockSpec((1, 2 * HEAD_DIM), lambda b, h: (0, 0)),
    ]
    out_spec = pl.BlockSpec((seq_len, 2 * HEAD_DIM), lambda b, h: (b, h))
    out = jax.ShapeDtypeStruct((rows, SECTION), BF16)
    return pl.pallas_call(
        functools.partial(_prompt_attn_body, lam_init=lam_init),
        grid=(nb, N_HEADS_A),
        in_specs=in_specs,
        out_specs=[out_spec, out_spec],
        out_shape=[out, out],
        compiler_params=_params(("parallel", "parallel")),
        name="prompt_attn",
    )(lam_vecs, qkv, qkv, qkv, qkv, qkv, qkv, qkv, qkv, subln_g)


DIFF_KEY_CHUNK = 1024


def _latent_diff_body(lam_ref, q0, q1, k0, k1, v, kc0, kc1, vc, g_ref, o_ref,
                      *, lam_init):
    lam = _lam_value(lam_ref, lam_init)
    v_ctx = vc[...].astype(BF16)
    n_lat = k0.shape[0]

    def attend(q, k, kc):
        qv = q[...]
        s = _scores(qv, kc[...])
        m = jnp.max(s, axis=-1, keepdims=True)
        e = jnp.exp2(s - m)
        denom = jnp.sum(e, axis=-1, keepdims=True)
        acc = jnp.dot(e.astype(BF16), v_ctx, preferred_element_type=F32)
        for c in range(n_lat // DIFF_KEY_CHUNK):
            keys = slice(c * DIFF_KEY_CHUNK, (c + 1) * DIFF_KEY_CHUNK)
            s = _scores(qv, k[keys, :])
            m_new = jnp.maximum(m, jnp.max(s, axis=-1, keepdims=True))
            alpha = jnp.exp2(m - m_new)
            e = jnp.exp2(s - m_new)
            denom = denom * alpha + jnp.sum(e, axis=-1, keepdims=True)
            acc = acc * alpha + jnp.dot(e.astype(BF16), v[keys, :], preferred_element_type=F32)
            m = m_new
        return acc / denom

    o = attend(q0, k0, kc0) - lam * attend(q1, k1, kc1)
    o_ref[...] = _subln(o, g_ref, lam_init).astype(o_ref.dtype)


def _latent_diff_attn(qkv, cache_k, cache_v, lam_vecs, subln_g, lam_init,
                      seq_len, ctx_len):
    rows = qkv.shape[0]
    nb = rows // seq_len
    tq = 512
    nq = seq_len // tq
    sec128 = SECTION // HEAD_DIM
    sec256 = SECTION // (2 * HEAD_DIM)
    half = sec128 // 2

    def qblk(col0):
        return pl.BlockSpec((tq, HEAD_DIM), lambda b, h, i: (b * nq + i, col0 + h))

    def kvblk(n, width, col0):
        return pl.BlockSpec((n, width), lambda b, h, i: (b, col0 + h))

    in_specs = [
        pl.BlockSpec((4, HEAD_DIM), lambda b, h, i: (0, 0)),
        qblk(0), qblk(half),
        kvblk(seq_len, HEAD_DIM, sec128), kvblk(seq_len, HEAD_DIM, sec128 + half),
        kvblk(seq_len, 2 * HEAD_DIM, 2 * sec256),
        kvblk(ctx_len, HEAD_DIM, 0), kvblk(ctx_len, HEAD_DIM, half),
        kvblk(ctx_len, 2 * HEAD_DIM, 0),
        pl.BlockSpec((1, 2 * HEAD_DIM), lambda b, h, i: (0, 0)),
    ]
    return pl.pallas_call(
        functools.partial(_latent_diff_body, lam_init=lam_init),
        grid=(nb, N_HEADS_A, nq),
        in_specs=in_specs,
        out_specs=pl.BlockSpec((tq, 2 * HEAD_DIM), lambda b, h, i: (b * nq + i, h)),
        out_shape=jax.ShapeDtypeStruct((rows, SECTION), BF16),
        compiler_params=_params(("parallel", "parallel", "arbitrary")),
        name="latent_diff_attn",
    )(lam_vecs, qkv, qkv, qkv, qkv, qkv, cache_k, cache_k, cache_v, subln_g)


NA_Q_ROWS = 8
NA_K_ROWS = 16


def _na_band_start(i, grid_rows):
    return jnp.clip(i * NA_Q_ROWS - NA_ROWS // 2, 0, grid_rows - NA_K_ROWS)


NA_D_MIN = -NA_Q_ROWS
NA_D_MAX = NA_K_ROWS + NA_ROWS - 2


def _na_pair_tables(rpb):
    n_h, n_dr, n_dc = rpb.shape
    u = jnp.concatenate([rpb[..., NA_COLS - 1:],
                         jnp.zeros((n_h, n_dr, LANE - n_dc), F32),
                         rpb[..., :NA_COLS - 1]], axis=-1)
    skew = jnp.tile(u, (1, 1, GRID_W))[..., :GRID_W * (LANE - 1)]
    toep = skew.reshape(n_h, n_dr, GRID_W, LANE - 1)[..., :GRID_W]
    cols = jnp.arange(GRID_W)
    cstart = jnp.clip(cols - NA_COLS // 2, 0, GRID_W - NA_COLS)
    col_ok = (cols[None, :] >= cstart[:, None]) & (cols[None, :] < cstart[:, None] + NA_COLS)
    toep = jnp.where(col_ok, toep, MASK_VALUE)
    ext = jnp.pad(toep, ((0, 0), (-NA_D_MIN, NA_D_MAX + 1 - n_dr), (0, 0), (0, 0)),
                  constant_values=MASK_VALUE)
    return jnp.concatenate([ext[:, :-1], ext[:, 1:]], axis=-1)


def _latent_na_body(q, k, v, kc, vc, tab_ref, o_ref, bias_ref, *, grid_rows):
    i = pl.program_id(2)
    n_blocks = grid_rows // NA_Q_ROWS
    row0 = i * NA_Q_ROWS
    band0 = _na_band_start(i, grid_rows)

    @pl.when(jnp.logical_or(i <= 1, i == n_blocks - 1))
    def _():
        lane = lax.broadcasted_iota(jnp.int32, (GRID_W, LANE), 1)
        for rq in range(NA_Q_ROWS):
            r = row0 + rq
            rs = jnp.clip(r - NA_ROWS // 2, 0, grid_rows - NA_ROWS)

            def penalty(kr):
                ok = jnp.logical_and(kr >= rs, kr < rs + NA_ROWS)
                return jnp.where(ok, 0.0, MASK_VALUE)

            for p in range(NA_K_ROWS // 2):
                kr = band0 + 2 * p
                d = kr - r + NA_ROWS - 1
                pen = jnp.where(lane < GRID_W, penalty(kr), penalty(kr + 1))
                bias_ref[rq * GRID_W:(rq + 1) * GRID_W, p * LANE:(p + 1) * LANE] = (
                    tab_ref[d - NA_D_MIN] * LOG2E + pen)

    start = pl.multiple_of(band0 * GRID_W, GRID_W)
    band = pl.ds(start, NA_K_ROWS * GRID_W)
    k_band, v_band = k[band, :], v[band, :]
    k_ctx, v_ctx = kc[...], vc[...].astype(BF16)
    half = q.shape[0] // 2
    for part in range(2):
        rows = slice(part * half, (part + 1) * half)
        e_loc, e_ctx, denom = _joint_exp(_scores(q[rows, :], k_band) + bias_ref[rows, :],
                                         _scores(q[rows, :], k_ctx))
        o = (jnp.dot(e_loc, v_band, preferred_element_type=F32)
             + jnp.dot(e_ctx, v_ctx, preferred_element_type=F32))
        o_ref[rows, :] = (o / denom).astype(o_ref.dtype)


def _latent_na_attn(qkv, cache_k, cache_v, rpb, seq_len, ctx_len):
    rows = qkv.shape[0]
    nb = rows // seq_len
    grid_rows = seq_len // GRID_W
    n_blocks = grid_rows // NA_Q_ROWS
    tq = NA_Q_ROWS * GRID_W
    sec128 = SECTION // HEAD_DIM
    tables = _na_pair_tables(rpb)

    def kvblk(n, col0):
        return pl.BlockSpec((n, HEAD_DIM), lambda h, b, i: (b, col0 + h))

    in_specs = [
        pl.BlockSpec((tq, HEAD_DIM), lambda h, b, i: (b * n_blocks + i, 3 * sec128 + h)),
        kvblk(seq_len, 4 * sec128), kvblk(seq_len, 5 * sec128),
        kvblk(ctx_len, 0), kvblk(ctx_len, 0),
        pl.BlockSpec((None,) + tables.shape[1:], lambda h, b, i: (h, 0, 0, 0)),
    ]
    return pl.pallas_call(
        functools.partial(_latent_na_body, grid_rows=grid_rows),
        grid=(N_HEADS_B, nb, n_blocks),
        in_specs=in_specs,
        out_specs=pl.BlockSpec((tq, HEAD_DIM), lambda h, b, i: (b * n_blocks + i, h)),
        out_shape=jax.ShapeDtypeStruct((rows, SECTION), BF16),
        scratch_shapes=[pltpu.VMEM((tq, NA_K_ROWS * GRID_W), F32)],
        compiler_params=_params(("arbitrary", "arbitrary", "arbitrary")),
        name="latent_na_attn",
    )(qkv, qkv, qkv, cache_k, cache_v, tables)


def _spatial_gate_body(u_ref, v_ref, g_ref, ws_ref, bs_ref, o_ref):
    vn = (_rms(v_ref[...]) * g_ref[...]).astype(BF16)
    width = vn.shape[1] // N_GROUPS
    for g in range(N_GROUPS):
        cols = slice(g * width, (g + 1) * width)
        sv = jnp.dot(ws_ref[g].astype(BF16), vn[:, cols],
                     preferred_element_type=F32) + bs_ref[:, g:g + 1]
        o_ref[:, cols] = (u_ref[:, cols] * sv).astype(o_ref.dtype)


def _spatial_gate(z, vnorm_g, w_sp, b_sp, layer):
    rows, two_d = z.shape
    d = two_d // 2
    return pl.pallas_call(
        _spatial_gate_body,
        grid=(rows // CHUNK,),
        in_specs=[pl.BlockSpec((CHUNK, d), lambda n: (n, 0)),
                  pl.BlockSpec((CHUNK, d), lambda n: (n, 1)),
                  pl.BlockSpec((None, 1, d), lambda n: (layer, 0, 0)),
                  pl.BlockSpec((None, N_GROUPS, CHUNK, CHUNK), lambda n: (layer, 0, 0, 0)),
                  pl.BlockSpec((None, CHUNK, N_GROUPS), lambda n: (layer, 0, 0))],
        out_specs=pl.BlockSpec((CHUNK, d), lambda n: (n, 0)),
        out_shape=jax.ShapeDtypeStruct((rows, d), BF16),
        compiler_params=_params(("parallel",)),
        name="spatial_gate",
    )(z, z, vnorm_g.reshape(vnorm_g.shape[0], 1, d), w_sp, jnp.swapaxes(b_sp, 1, 2))


FFN_TN = 256
HALO = 32
SUB = 8
FFN_TM = 2048
FFN_CHUNK = 256


def _ffn_front_body(x_ref, xp_ref, xn_ref, wg_ref, wu_ref, cwg_ref, cwu_ref,
                    cbg_ref, cbu_ref, o_ref, acc_ref, *, seq_len):
    tm = x_ref.shape[0]
    i = pl.program_id(0)
    n = FFN_CHUNK
    n_chunks = tm // n
    tn = wg_ref.shape[1]
    w = jnp.concatenate([wg_ref[...].astype(BF16), wu_ref[...].astype(BF16)], axis=1)

    edges = jnp.concatenate([xp_ref[...], xn_ref[...]], axis=0)
    halo = jnp.dot(edges, w, preferred_element_type=F32)
    acc_ref[0:HALO, :] = halo[:HALO]
    acc_ref[HALO + tm:, :] = halo[HALO:]

    def project(c):
        acc_ref[HALO + c * n:HALO + (c + 1) * n, :] = jnp.dot(
            x_ref[c * n:(c + 1) * n, :], w, preferred_element_type=F32)

    def conv(col0, cw_ref, cb_ref, c, at_start, at_end):
        r0 = HALO + c * n
        cols = slice(col0, col0 + tn)
        x_prev = acc_ref[r0 - 1:r0 - 1 + n, cols]
        x_next = acc_ref[r0 + 1:r0 + 1 + n, cols]
        row = lax.broadcasted_iota(jnp.int32, (SUB, 1), 0)
        head = jnp.where(jnp.logical_and(at_start, row == 0), 0.0, x_prev[:SUB])
        tail = jnp.where(jnp.logical_and(at_end, row == SUB - 1), 0.0, x_next[n - SUB:])
        x_prev = jnp.concatenate([head, x_prev[SUB:]], axis=0)
        x_next = jnp.concatenate([x_next[:n - SUB], tail], axis=0)
        return (cw_ref[0:1, :] * x_prev + cw_ref[1:2, :] * acc_ref[r0:r0 + n, cols]
                + cw_ref[2:3, :] * x_next + cb_ref[...])

    def gate(c):
        first = i * tm + c * n
        at_start = first % seq_len == 0
        at_end = (first + n) % seq_len == 0
        g = conv(0, cwg_ref, cbg_ref, c, at_start, at_end)
        u = conv(tn, cwu_ref, cbu_ref, c, at_start, at_end)
        o_ref[c * n:(c + 1) * n, :] = (g * jax.nn.sigmoid(g) * u).astype(o_ref.dtype)

    project(0)
    for c in range(n_chunks):
        if c + 1 < n_chunks:
            project(c + 1)
        gate(c)


def _ffn_front(h, w_up, conv_w, conv_b, layer, group):
    rows, d = h.shape
    two_f = w_up.shape[-1]
    d_ff = two_f // 2
    tm, tn = min(FFN_TM, rows), FFN_TN
    nj = d_ff // tn
    assert rows % tm == 0 and tm % FFN_CHUNK == 0 and d_ff % tn == 0
    assert group.seq_len % FFN_CHUNK == 0 and (tm % group.seq_len == 0 or group.seq_len % tm == 0)
    halo_blocks = rows // HALO

    def w(off):
        return pl.BlockSpec((None, d, tn), lambda i, j: (layer, 0, off + j))

    def cw(off):
        return pl.BlockSpec((None, 3, tn), lambda i, j: (layer, 0, off + j))

    def cb(off):
        return pl.BlockSpec((None, 1, tn), lambda i, j: (layer, 0, off + j))

    in_specs = [
        pl.BlockSpec((tm, d), lambda i, j: (i, 0), pipeline_mode=pl.Buffered(1)),
        pl.BlockSpec((HALO, d), lambda i, j: (jnp.maximum(i * (tm // HALO) - 1, 0), 0)),
        pl.BlockSpec((HALO, d), lambda i, j: (jnp.minimum((i + 1) * (tm // HALO), halo_blocks - 1), 0)),
        w(0), w(nj), cw(0), cw(nj), cb(0), cb(nj),
    ]
    cb3 = conv_b.reshape(conv_b.shape[0], 1, two_f)
    return pl.pallas_call(
        functools.partial(_ffn_front_body, seq_len=group.seq_len),
        grid=(rows // tm, nj),
        in_specs=in_specs,
        out_specs=pl.BlockSpec((tm, tn), lambda i, j: (i, j)),
        out_shape=jax.ShapeDtypeStruct((rows, d_ff), BF16),
        scratch_shapes=[pltpu.VMEM((tm + 2 * HALO, 2 * tn), F32)],
        compiler_params=_params(("parallel", "arbitrary")),
        name="ffn_front",
    )(h, h, h, w_up, w_up, conv_w, conv_w, cb3, cb3)


def kernel(x_prompt, x_sample, cache_a_k, cache_a_v, cache_b_k, cache_b_v, c, c_ctx,
           w_ada, b_ada, norm1_g, norm2_g,
           w_in_even, qn_a, kn_a, lam_q1, lam_k1, lam_q2, lam_k2, subln_g, qn_b, kn_b, rpb, w_out_even,
           w_in_odd, vnorm_g, w_sp, b_sp, w_out_odd,
           w_up, conv_w, conv_b, w_down):
    batch, seq, d = x_prompt.shape
    dec_batch, dec_seq, _ = x_sample.shape
    past_len = cache_a_k.shape[2]
    depth = w_ada.shape[0]
    n_even = w_in_even.shape[0]

    groups = (_Group(batch * seq, seq, 0), _Group(dec_batch * dec_seq, dec_seq, 1))
    xs = [x_prompt.reshape(batch * seq, d), x_sample.reshape(dec_batch * dec_seq, d)]

    cond8 = jnp.concatenate([c_ctx[None], c, jnp.zeros((8 - 1 - dec_batch, d), F32)])

    w_in_even_b, w_out_even_b, w_in_odd_b, w_out_odd_b, w_up_b = (
        w_in_even, w_out_even, w_in_odd, w_out_odd, w_up)
    w_down_b = w_down.astype(BF16)

    prompt_qkv = []
    for layer in range(depth):
        mod = _adaln(cond8, w_ada, b_ada, layer).reshape(8, 6, 1, d)
        if layer % 2 == 0:
            e = layer // 2
            lam_init = 0.8 - 0.6 * math.exp(-0.3 * layer)
            lam_vecs = jnp.stack([lam_q1[e], lam_k1[e], lam_q2[e], lam_k2[e]])
            ones = jnp.ones((SECTION,), F32)
            reps = SECTION // HEAD_DIM
            gains = jnp.concatenate([jnp.tile(qn_a[e] * Q_PRESCALE, reps), jnp.tile(kn_a[e], reps), ones,
                                     jnp.tile(qn_b[e] * Q_PRESCALE, reps), jnp.tile(kn_b[e], reps), ones])[None]
            sub_g = subln_g[e][None]
            mixed = []
            for gi, group in enumerate(groups):
                h = _norm_mod(xs[gi], norm1_g, layer, mod, 0, group)
                if gi == 0:
                    qkv = _qkv_proj(h, w_in_even_b, e, gains, group, rope=False, out_dtype=F32)
                    prompt_qkv.append(qkv)
                    oa, ob = _prompt_attn(qkv, lam_vecs, sub_g, lam_init, group.seq_len)
                else:
                    qkv = _qkv_proj(h, w_in_even_b, e, gains, group, rope=True, out_dtype=BF16)
                    ck_a = cache_a_k[:, e].reshape(dec_batch * past_len, SECTION)
                    cv_a = cache_a_v[:, e].reshape(dec_batch * past_len, SECTION)
                    ck_b = cache_b_k[:, e].reshape(dec_batch * past_len, SECTION)
                    cv_b = cache_b_v[:, e].reshape(dec_batch * past_len, SECTION)
                    oa = _latent_diff_attn(qkv, ck_a, cv_a, lam_vecs, sub_g, lam_init,
                                           group.seq_len, past_len)
                    ob = _latent_na_attn(qkv, ck_b, cv_b, rpb[e], group.seq_len, past_len)
                mixed.append((oa, ob))
            for gi, group in enumerate(groups):
                xs[gi] = _matmul_resid(mixed[gi], w_out_even_b, e, xs[gi], mod, 2, group,
                                       name="even_out_proj")
        else:
            o = layer // 2
            for gi, group in enumerate(groups):
                h = _norm_mod(xs[gi], norm1_g, layer, mod, 0, group)
                z = _matmul(h, w_in_odd_b, o, _gelu_epilogue, (), [], F32, name="odd_in_proj")
                gated = _spatial_gate(z, vnorm_g, w_sp, b_sp, o)
                xs[gi] = _matmul_resid(gated, w_out_odd_b, o, xs[gi], mod, 2, group,
                                       name="odd_out_proj")
        for gi, group in enumerate(groups):
            h = _norm_mod(xs[gi], norm2_g, layer, mod, 3, group)
            act = _ffn_front(h, w_up_b, conv_w, conv_b, layer, group)
            half = _matmul(act, w_down_b, layer, _plain_epilogue, (), [], F32,
                           k_parts=2, k_part=0, name="ffn_down_lo")
            xs[gi] = _matmul_resid(act, w_down_b, layer, xs[gi], mod, 5, group, partial_sum=half,
                                   k_parts=2, k_part=1, name="ffn_down_hi")

    def state(col0, shape):
        parts = [q[:, col0:col0 + SECTION].reshape((batch, seq) + shape) for q in prompt_qkv]
        return jnp.stack(parts, axis=1)

    h_a, h_b = N_HEADS_A, N_HEADS_B
    return (xs[0].reshape(batch, seq, d),
            xs[1].reshape(dec_batch, dec_seq, d),
            state(1 * SECTION, (2, h_a, HEAD_DIM)),
            state(2 * SECTION, (h_a, 2 * HEAD_DIM)),
            state(4 * SECTION, (h_b, HEAD_DIM)),
            state(5 * SECTION, (h_b, HEAD_DIM)))
```

```python
import functools
import math

import jax
import jax.numpy as jnp
from jax import lax
from jax.experimental import pallas as pl
from jax.experimental.pallas import tpu as pltpu

F32 = jnp.float32
BF16 = jnp.bfloat16

NORM_EPS = 1e-6
ROPE_BASE = 10000.0
GRID_W = 64
HEAD_DIM = 128
N_HEADS_A = 8
N_HEADS_B = 16
SECTION = 2048
NA_ROWS = 8
NA_COLS = 16
CHUNK = 128
N_GROUPS = 8
MASK_VALUE = -1e30
LOG2E = math.log2(math.e)
Q_PRESCALE = HEAD_DIM ** -0.5 * LOG2E
LANE = 128
VMEM_LIMIT = 56 * 1024 * 1024

_NT = (((1,), (1,)), ((), ()))


def _params(semantics):
    return pltpu.CompilerParams(dimension_semantics=semantics,
                                vmem_limit_bytes=VMEM_LIMIT)


def _rms(x):
    return x * lax.rsqrt(jnp.mean(x * x, axis=-1, keepdims=True) + NORM_EPS)


def _adaln_body(c_ref, w_ref, b_ref, o_ref):
    c = c_ref[...]
    s = c * jax.nn.sigmoid(c)
    o_ref[...] = jnp.dot(s.astype(BF16), w_ref[...].astype(BF16),
                         preferred_element_type=F32) + b_ref[...]


def _adaln(cond8, w_ada, b_ada, layer):
    _, d, n = w_ada.shape
    tn = 1024
    return pl.pallas_call(
        _adaln_body,
        grid=(n // tn,),
        in_specs=[pl.BlockSpec((8, d), lambda j: (0, 0)),
                  pl.BlockSpec((None, d, tn), lambda j: (layer, 0, j)),
                  pl.BlockSpec((None, 1, tn), lambda j: (layer, 0, j))],
        out_specs=pl.BlockSpec((8, tn), lambda j: (0, j)),
        out_shape=jax.ShapeDtypeStruct((8, n), F32),
        compiler_params=_params(("parallel",)),
        name="adaln",
    )(cond8, w_ada, b_ada.reshape(b_ada.shape[0], 1, n))


class _Group:
    def __init__(self, rows, seq_len, cond_base):
        self.rows = rows
        self.seq_len = seq_len
        self.cond_base = cond_base

    def cond(self, row0):
        if self.cond_base == 0:
            return 0
        return self.cond_base + row0 // self.seq_len


def _mod_spec(group, tm, tn, which, col_of):
    return pl.BlockSpec(
        (None, None, 1, tn),
        lambda *ids: (group.cond(ids[0] * tm), which, 0, col_of(*ids)))


def _norm_mod_body(x_ref, g_ref, sh_ref, sc_ref, o_ref):
    y = _rms(x_ref[...]) * g_ref[...]
    o_ref[...] = (y * (1 + sc_ref[...]) + sh_ref[...]).astype(o_ref.dtype)


def _norm_mod(x, gains, layer, mod, shift_idx, group):
    rows, d = x.shape
    tm = 256
    return pl.pallas_call(
        _norm_mod_body,
        grid=(rows // tm,),
        in_specs=[pl.BlockSpec((tm, d), lambda i: (i, 0)),
                  pl.BlockSpec((None, 1, d), lambda i: (layer, 0, 0)),
                  _mod_spec(group, tm, d, shift_idx, lambda i: 0),
                  _mod_spec(group, tm, d, shift_idx + 1, lambda i: 0)],
        out_specs=pl.BlockSpec((tm, d), lambda i: (i, 0)),
        out_shape=jax.ShapeDtypeStruct((rows, d), BF16),
        compiler_params=_params(("parallel",)),
        name="norm_mod",
    )(x, gains.reshape(gains.shape[0], 1, d), mod, mod)


MM_TM = 1024
MM_TN = 512
MM_CHUNK = 256


def _mm_body(*refs, n_x, n_extra, epilogue):
    x_refs, w_ref = refs[:n_x], refs[n_x]
    extras = refs[n_x + 1:n_x + 1 + n_extra]
    o_ref = refs[n_x + 1 + n_extra]
    j = pl.program_id(1)
    w = w_ref[...].astype(BF16)
    tm = x_refs[0].shape[0]
    sizes = (MM_CHUNK,) * (tm // MM_CHUNK - 1) + (MM_CHUNK // 2,) * 2
    for c in range(len(sizes)):
        rows = slice(sum(sizes[:c]), sum(sizes[:c + 1]))
        acc, k0 = None, 0
        for x_ref in x_refs:
            k1 = k0 + x_ref.shape[1]
            part = jnp.dot(x_ref[rows, :], w[k0:k1, :], preferred_element_type=F32)
            acc = part if acc is None else acc + part
            k0 = k1
        epilogue(acc, extras, o_ref, j, rows)


def _matmul(x, w, layer, epilogue, extras, extra_specs, out_dtype, *,
            k_parts=1, k_part=0, name):
    xs = x if isinstance(x, tuple) else (x,)
    assert len(xs) == 1 or k_parts == 1
    rows = xs[0].shape[0]
    kdim = sum(p.shape[1] for p in xs)
    n = w.shape[-1]
    tm, tn = MM_TM, MM_TN
    tk = kdim // k_parts
    body = functools.partial(_mm_body, n_x=len(xs), n_extra=len(extras), epilogue=epilogue)
    x_specs = [pl.BlockSpec((tm, p.shape[1] // k_parts), lambda i, j: (i, k_part)) for p in xs]
    return pl.pallas_call(
        body,
        grid=(rows // tm, n // tn),
        in_specs=[*x_specs,
                  pl.BlockSpec((None, tk, tn), lambda i, j: (layer, k_part, j)),
                  *extra_specs],
        out_specs=pl.BlockSpec((tm, tn), lambda i, j: (i, j)),
        out_shape=jax.ShapeDtypeStruct((rows, n), out_dtype),
        compiler_params=_params(("parallel", "arbitrary")),
        name=name,
    )(*xs, w, *extras)


def _plain_epilogue(acc, extras, o_ref, j, rows):
    o_ref[rows, :] = acc.astype(o_ref.dtype)


def _gelu_epilogue(acc, extras, o_ref, j, rows):
    o_ref[rows, :] = (0.5 * acc * (1 + lax.erf(acc * math.sqrt(0.5)))).astype(o_ref.dtype)


def _resid_epilogue(acc, extras, o_ref, j, rows):
    x_ref, gate_ref = extras[:2]
    if len(extras) == 3:
        acc = extras[2][rows, :] + acc
    o_ref[rows, :] = x_ref[rows, :] + gate_ref[...] * acc


def _matmul_resid(h, w, layer, x, mod, gate_idx, group, *, partial_sum=None,
                  k_parts=1, k_part=0, name):
    tile = pl.BlockSpec((MM_TM, MM_TN), lambda i, j: (i, j))
    specs = [tile, _mod_spec(group, MM_TM, MM_TN, gate_idx, lambda i, j: j)]
    extras = (x, mod)
    if partial_sum is not None:
        specs.append(tile)
        extras += (partial_sum,)
    return _matmul(h, w, layer, _resid_epilogue, extras, specs, F32,
                   k_parts=k_parts, k_part=k_part, name=name)


def _qkv_epilogue(acc, extras, o_ref, j, rows, *, rope):
    g_ref = extras[0]
    sec = (j * MM_TN) // SECTION
    is_v = jnp.logical_or(sec == 2, sec == 5)
    if rope:
        is_rope = sec < 2
        cos = jnp.where(is_rope, extras[1][rows, :], 1.0)
        sin_lo = jnp.where(is_rope, extras[2][rows, :], 0.0)
        sin_hi = jnp.where(is_rope, extras[3][rows, :], 0.0)
    for c in range(MM_TN // LANE):
        cols = slice(c * LANE, (c + 1) * LANE)
        a = acc[:, cols]
        y = jnp.where(is_v, a, _rms(a) * g_ref[:, cols])
        if rope:
            y = (y * cos + pltpu.roll(y, LANE - 32, 1) * sin_lo
                 + pltpu.roll(y, 32, 1) * sin_hi)
        o_ref[rows, cols] = y.astype(o_ref.dtype)


def _rope_tables(n_tok):
    quarter = HEAD_DIM // 4
    t = jnp.arange(n_tok)
    inv = ROPE_BASE ** (-jnp.arange(quarter, dtype=F32) / quarter)

    def ang(p):
        a = p.astype(F32)[:, None] * inv[None, :]
        return jnp.concatenate([a, a], axis=-1)

    angles = jnp.concatenate([ang(t // GRID_W), ang(t % GRID_W)], axis=-1)
    cos, sin = jnp.cos(angles), jnp.sin(angles)
    low = (jnp.arange(HEAD_DIM) % (2 * quarter)) < quarter
    return cos, jnp.where(low, -sin, 0.0), jnp.where(low, 0.0, sin)


def _qkv_proj(h, w, layer, gains, group, *, rope, out_dtype):
    extras = [gains]
    specs = [pl.BlockSpec((1, MM_TN), lambda i, j: (0, j))]
    if rope:
        tiles_per_seq = group.seq_len // MM_TM
        tab = pl.BlockSpec((MM_TM, HEAD_DIM), lambda i, j: (i % tiles_per_seq, 0))
        extras += list(_rope_tables(group.seq_len))
        specs += [tab, tab, tab]
    epi = functools.partial(_qkv_epilogue, rope=rope)
    return _matmul(h, w, layer, epi, tuple(extras), specs, out_dtype, name="qkv_proj")


def _lam_value(lam_ref, lam_init):
    v = lam_ref[...]
    t1 = jnp.sum(v[0:1] * v[1:2], axis=-1, keepdims=True)
    t2 = jnp.sum(v[2:3] * v[3:4], axis=-1, keepdims=True)
    return jnp.exp(t1) - jnp.exp(t2) + lam_init


def _scores(q, k):
    return lax.dot_general(q.astype(BF16), k.astype(BF16), _NT,
                           preferred_element_type=F32)


def _softmax(s):
    e = jnp.exp2(s - jnp.max(s, axis=-1, keepdims=True))
    return e / jnp.sum(e, axis=-1, keepdims=True)


def _joint_exp(s_a, s_b):
    m = jnp.maximum(jnp.max(s_a, axis=-1, keepdims=True),
                    jnp.max(s_b, axis=-1, keepdims=True))
    e_a = jnp.exp2(s_a - m)
    e_b = jnp.exp2(s_b - m)
    denom = (jnp.sum(e_a, axis=-1, keepdims=True)
             + jnp.sum(e_b, axis=-1, keepdims=True))
    return e_a.astype(BF16), e_b.astype(BF16), denom


def _subln(o, g_ref, lam_init):
    return _rms(o) * g_ref[...] * (1.0 - lam_init)


def _prompt_attn_body(lam_ref, q0, q1, k0, k1, va, qb, kb, vb, g_ref,
                      oa_ref, ob_ref, *, lam_init):
    lam = _lam_value(lam_ref, lam_init)
    a = _softmax(_scores(q0[...], k0[...])) - lam * _softmax(_scores(q1[...], k1[...]))
    o = jnp.dot(a.astype(BF16), va[...].astype(BF16), preferred_element_type=F32)
    oa_ref[...] = _subln(o, g_ref, lam_init).astype(oa_ref.dtype)
    for t in range(2):
        cols = slice(t * HEAD_DIM, (t + 1) * HEAD_DIM)
        p = _softmax(_scores(qb[:, cols], kb[:, cols]))
        ob_ref[:, cols] = jnp.dot(p.astype(BF16), vb[:, cols].astype(BF16),
                                  preferred_element_type=F32).astype(ob_ref.dtype)


def _prompt_attn(qkv, lam_vecs, subln_g, lam_init, seq_len):
    rows = qkv.shape[0]
    nb = rows // seq_len
    sec128 = SECTION // HEAD_DIM
    sec256 = SECTION // (2 * HEAD_DIM)

    def blk(width, col0):
        return pl.BlockSpec((seq_len, width), lambda b, h: (b, col0 + h))

    half = sec128 // 2
    in_specs = [
        pl.BlockSpec((4, HEAD_DIM), lambda b, h: (0, 0)),
        blk(HEAD_DIM, 0), blk(HEAD_DIM, half),
        blk(HEAD_DIM, sec128), blk(HEAD_DIM, sec128 + half),
        blk(2 * HEAD_DIM, 2 * sec256),
        blk(2 * HEAD_DIM, 3 * sec256),
        blk(2 * HEAD_DIM, 4 * sec256),
        blk(2 * HEAD_DIM, 5 * sec256),
        pl.BlockSpec((1, 2 * HEAD_DIM), lambda b, h: (0, 0)),
    ]
    out_spec = pl.BlockSpec((seq_len, 2 * HEAD_DIM), lambda b, h: (b, h))
    out = jax.ShapeDtypeStruct((rows, SECTION), BF16)
    return pl.pallas_call(
        functools.partial(_prompt_attn_body, lam_init=lam_init),
        grid=(nb, N_HEADS_A),
        in_specs=in_specs,
        out_specs=[out_spec, out_spec],
        out_shape=[out, out],
        compiler_params=_params(("parallel", "parallel")),
        name="prompt_attn",
    )(lam_vecs, qkv, qkv, qkv, qkv, qkv, qkv, qkv, qkv, subln_g)


DIFF_KEY_CHUNK = 1024


def _latent_diff_body(lam_ref, q0, q1, k0, k1, v, kc0, kc1, vc, g_ref, o_ref,
                      *, lam_init):
    lam = _lam_value(lam_ref, lam_init)
    v_ctx = vc[...].astype(BF16)
    n_lat = k0.shape[0]

    def attend(q, k, kc):
        qv = q[...]
        s = _scores(qv, kc[...])
        m = jnp.max(s, axis=-1, keepdims=True)
        e = jnp.exp2(s - m)
        denom = jnp.sum(e, axis=-1, keepdims=True)
        acc = jnp.dot(e.astype(BF16), v_ctx, preferred_element_type=F32)
        for c in range(n_lat // DIFF_KEY_CHUNK):
            keys = slice(c * DIFF_KEY_CHUNK, (c + 1) * DIFF_KEY_CHUNK)
            s = _scores(qv, k[keys, :])
            m_new = jnp.maximum(m, jnp.max(s, axis=-1, keepdims=True))
            alpha = jnp.exp2(m - m_new)
            e = jnp.exp2(s - m_new)
            denom = denom * alpha + jnp.sum(e, axis=-1, keepdims=True)
            acc = acc * alpha + jnp.dot(e.astype(BF16), v[keys, :], preferred_element_type=F32)
            m = m_new
        return acc / denom

    o = attend(q0, k0, kc0) - lam * attend(q1, k1, kc1)
    o_ref[...] = _subln(o, g_ref, lam_init).astype(o_ref.dtype)


def _latent_diff_attn(qkv, cache_k, cache_v, lam_vecs, subln_g, lam_init,
                      seq_len, ctx_len):
    rows = qkv.shape[0]
    nb = rows // seq_len
    tq = 512
    nq = seq_len // tq
    sec128 = SECTION // HEAD_DIM
    sec256 = SECTION // (2 * HEAD_DIM)
    half = sec128 // 2

    def qblk(col0):
        return pl.BlockSpec((tq, HEAD_DIM), lambda b, h, i: (b * nq + i, col0 + h))

    def kvblk(n, width, col0):
        return pl.BlockSpec((n, width), lambda b, h, i: (b, col0 + h))

    in_specs = [
        pl.BlockSpec((4, HEAD_DIM), lambda b, h, i: (0, 0)),
        qblk(0), qblk(half),
        kvblk(seq_len, HEAD_DIM, sec128), kvblk(seq_len, HEAD_DIM, sec128 + half),
        kvblk(seq_len, 2 * HEAD_DIM, 2 * sec256),
        kvblk(ctx_len, HEAD_DIM, 0), kvblk(ctx_len, HEAD_DIM, half),
        kvblk(ctx_len, 2 * HEAD_DIM, 0),
        pl.BlockSpec((1, 2 * HEAD_DIM), lambda b, h, i: (0, 0)),
    ]
    return pl.pallas_call(
        functools.partial(_latent_diff_body, lam_init=lam_init),
        grid=(nb, N_HEADS_A, nq),
        in_specs=in_specs,
        out_specs=pl.BlockSpec((tq, 2 * HEAD_DIM), lambda b, h, i: (b * nq + i, h)),
        out_shape=jax.ShapeDtypeStruct((rows, SECTION), BF16),
        compiler_params=_params(("parallel", "parallel", "arbitrary")),
        name="latent_diff_attn",
    )(lam_vecs, qkv, qkv, qkv, qkv, qkv, cache_k, cache_k, cache_v, subln_g)


NA_Q_ROWS = 8
NA_K_ROWS = 16


def _na_band_start(i, grid_rows):
    return jnp.clip(i * NA_Q_ROWS - NA_ROWS // 2, 0, grid_rows - NA_K_ROWS)


NA_D_MIN = -NA_Q_ROWS
NA_D_MAX = NA_K_ROWS + NA_ROWS - 2


def _na_pair_tables(rpb):
    n_h, n_dr, n_dc = rpb.shape
    u = jnp.concatenate([rpb[..., NA_COLS - 1:],
                         jnp.zeros((n_h, n_dr, LANE - n_dc), F32),
                         rpb[..., :NA_COLS - 1]], axis=-1)
    skew = jnp.tile(u, (1, 1, GRID_W))[..., :GRID_W * (LANE - 1)]
    toep = skew.reshape(n_h, n_dr, GRID_W, LANE - 1)[..., :GRID_W]
    cols = jnp.arange(GRID_W)
    cstart = jnp.clip(cols - NA_COLS // 2, 0, GRID_W - NA_COLS)
    col_ok = (cols[None, :] >= cstart[:, None]) & (cols[None, :] < cstart[:, None] + NA_COLS)
    toep = jnp.where(col_ok, toep, MASK_VALUE)
    ext = jnp.pad(toep, ((0, 0), (-NA_D_MIN, NA_D_MAX + 1 - n_dr), (0, 0), (0, 0)),
                  constant_values=MASK_VALUE)
    return jnp.concatenate([ext[:, :-1], ext[:, 1:]], axis=-1)


def _latent_na_body(q, k, v, kc, vc, tab_ref, o_ref, bias_ref, *, grid_rows):
    i = pl.program_id(2)
    n_blocks = grid_rows // NA_Q_ROWS
    row0 = i * NA_Q_ROWS
    band0 = _na_band_start(i, grid_rows)

    @pl.when(jnp.logical_or(i <= 1, i == n_blocks - 1))
    def _():
        lane = lax.broadcasted_iota(jnp.int32, (GRID_W, LANE), 1)
        for rq in range(NA_Q_ROWS):
            r = row0 + rq
            rs = jnp.clip(r - NA_ROWS // 2, 0, grid_rows - NA_ROWS)

            def penalty(kr):
                ok = jnp.logical_and(kr >= rs, kr < rs + NA_ROWS)
                return jnp.where(ok, 0.0, MASK_VALUE)

            for p in range(NA_K_ROWS // 2):
                kr = band0 + 2 * p
                d = kr - r + NA_ROWS - 1
                pen = jnp.where(lane < GRID_W, penalty(kr), penalty(kr + 1))
                bias_ref[rq * GRID_W:(rq + 1) * GRID_W, p * LANE:(p + 1) * LANE] = (
                    tab_ref[d - NA_D_MIN] * LOG2E + pen)

    start = pl.multiple_of(band0 * GRID_W, GRID_W)
    band = pl.ds(start, NA_K_ROWS * GRID_W)
    k_band, v_band = k[band, :], v[band, :]
    k_ctx, v_ctx = kc[...], vc[...].astype(BF16)
    half = q.shape[0] // 2
    for part in range(2):
        rows = slice(part * half, (part + 1) * half)
        e_loc, e_ctx, denom = _joint_exp(_scores(q[rows, :], k_band) + bias_ref[rows, :],
                                         _scores(q[rows, :], k_ctx))
        o = (jnp.dot(e_loc, v_band, preferred_element_type=F32)
             + jnp.dot(e_ctx, v_ctx, preferred_element_type=F32))
        o_ref[rows, :] = (o / denom).astype(o_ref.dtype)


def _latent_na_attn(qkv, cache_k, cache_v, rpb, seq_len, ctx_len):
    rows = qkv.shape[0]
    nb = rows // seq_len
    grid_rows = seq_len // GRID_W
    n_blocks = grid_rows // NA_Q_ROWS
    tq = NA_Q_ROWS * GRID_W
    sec128 = SECTION // HEAD_DIM
    tables = _na_pair_tables(rpb)

    def kvblk(n, col0):
        return pl.BlockSpec((n, HEAD_DIM), lambda h, b, i: (b, col0 + h))

    in_specs = [
        pl.BlockSpec((tq, HEAD_DIM), lambda h, b, i: (b * n_blocks + i, 3 * sec128 + h)),
        kvblk(seq_len, 4 * sec128), kvblk(seq_len, 5 * sec128),
        kvblk(ctx_len, 0), kvblk(ctx_len, 0),
        pl.BlockSpec((None,) + tables.shape[1:], lambda h, b, i: (h, 0, 0, 0)),
    ]
    return pl.pallas_call(
        functools.partial(_latent_na_body, grid_rows=grid_rows),
        grid=(N_HEADS_B, nb, n_blocks),
        in_specs=in_specs,
        out_specs=pl.BlockSpec((tq, HEAD_DIM), lambda h, b, i: (b * n_blocks + i, h)),
        out_shape=jax.ShapeDtypeStruct((rows, SECTION), BF16),
        scratch_shapes=[pltpu.VMEM((tq, NA_K_ROWS * GRID_W), F32)],
        compiler_params=_params(("arbitrary", "arbitrary", "arbitrary")),
        name="latent_na_attn",
    )(qkv, qkv, qkv, cache_k, cache_v, tables)


def _spatial_gate_body(u_ref, v_ref, g_ref, ws_ref, bs_ref, o_ref):
    vn = (_rms(v_ref[...]) * g_ref[...]).astype(BF16)
    width = vn.shape[1] // N_GROUPS
    for g in range(N_GROUPS):
        cols = slice(g * width, (g + 1) * width)
        sv = jnp.dot(ws_ref[g].astype(BF16), vn[:, cols],
                     preferred_element_type=F32) + bs_ref[:, g:g + 1]
        o_ref[:, cols] = (u_ref[:, cols] * sv).astype(o_ref.dtype)


def _spatial_gate(z, vnorm_g, w_sp, b_sp, layer):
    rows, two_d = z.shape
    d = two_d // 2
    return pl.pallas_call(
        _spatial_gate_body,
        grid=(rows // CHUNK,),
        in_specs=[pl.BlockSpec((CHUNK, d), lambda n: (n, 0)),
                  pl.BlockSpec((CHUNK, d), lambda n: (n, 1)),
                  pl.BlockSpec((None, 1, d), lambda n: (layer, 0, 0)),
                  pl.BlockSpec((None, N_GROUPS, CHUNK, CHUNK), lambda n: (layer, 0, 0, 0)),
                  pl.BlockSpec((None, CHUNK, N_GROUPS), lambda n: (layer, 0, 0))],
        out_specs=pl.BlockSpec((CHUNK, d), lambda n: (n, 0)),
        out_shape=jax.ShapeDtypeStruct((rows, d), BF16),
        compiler_params=_params(("parallel",)),
        name="spatial_gate",
    )(z, z, vnorm_g.reshape(vnorm_g.shape[0], 1, d), w_sp, jnp.swapaxes(b_sp, 1, 2))


FFN_TN = 256
HALO = 32
SUB = 8
FFN_TM = 2048
FFN_CHUNK = 256


def _ffn_chunks(tm):
    return (FFN_CHUNK,) * (tm // FFN_CHUNK - 1) + (FFN_CHUNK // 2,) * 2


def _ffn_front_body(x_ref, xp_ref, xn_ref, wg_ref, wu_ref, cwg_ref, cwu_ref,
                    cbg_ref, cbu_ref, o_ref, acc_ref, *, seq_len):
    tm = x_ref.shape[0]
    i = pl.program_id(0)
    sizes = _ffn_chunks(tm)
    starts = [sum(sizes[:c]) for c in range(len(sizes) + 1)]
    n_chunks = len(sizes)
    tn = wg_ref.shape[1]
    w = jnp.concatenate([wg_ref[...].astype(BF16), wu_ref[...].astype(BF16)], axis=1)

    def project(c):
        x = x_ref[starts[c]:starts[c + 1], :]
        lo, hi = HALO + starts[c], HALO + starts[c + 1]
        if c == 0:
            x, lo = jnp.concatenate([xp_ref[...], x], axis=0), 0
        if c == n_chunks - 1:
            x, hi = jnp.concatenate([x, xn_ref[...]], axis=0), tm + 2 * HALO
        acc_ref[lo:hi, :] = jnp.dot(x, w, preferred_element_type=F32)

    def conv(col0, cw_ref, cb_ref, c, at_start, at_end):
        n = sizes[c]
        r0 = HALO + starts[c]
        cols = slice(col0, col0 + tn)
        x_prev = acc_ref[r0 - 1:r0 - 1 + n, cols]
        x_next = acc_ref[r0 + 1:r0 + 1 + n, cols]
        row = lax.broadcasted_iota(jnp.int32, (SUB, 1), 0)
        head = jnp.where(jnp.logical_and(at_start, row == 0), 0.0, x_prev[:SUB])
        tail = jnp.where(jnp.logical_and(at_end, row == SUB - 1), 0.0, x_next[n - SUB:])
        x_prev = jnp.concatenate([head, x_prev[SUB:]], axis=0)
        x_next = jnp.concatenate([x_next[:n - SUB], tail], axis=0)
        return (cw_ref[0:1, :] * x_prev + cw_ref[1:2, :] * acc_ref[r0:r0 + n, cols]
                + cw_ref[2:3, :] * x_next + cb_ref[...])

    def gate(c):
        first = i * tm + starts[c]
        at_start = first % seq_len == 0
        at_end = (first + sizes[c]) % seq_len == 0
        g = conv(0, cwg_ref, cbg_ref, c, at_start, at_end)
        u = conv(tn, cwu_ref, cbu_ref, c, at_start, at_end)
        o_ref[starts[c]:starts[c + 1], :] = (g * jax.nn.sigmoid(g) * u).astype(o_ref.dtype)

    project(0)
    for c in range(n_chunks):
        if c + 1 < n_chunks:
            project(c + 1)
        gate(c)


def _ffn_front(h, w_up, conv_w, conv_b, layer, group):
    rows, d = h.shape
    two_f = w_up.shape[-1]
    d_ff = two_f // 2
    tm, tn = min(FFN_TM, rows), FFN_TN
    nj = d_ff // tn
    assert rows % tm == 0 and tm % FFN_CHUNK == 0 and d_ff % tn == 0
    assert group.seq_len % FFN_CHUNK == 0 and (tm % group.seq_len == 0 or group.seq_len % tm == 0)
    halo_blocks = rows // HALO

    def w(off):
        return pl.BlockSpec((None, d, tn), lambda i, j: (layer, 0, off + j))

    def cw(off):
        return pl.BlockSpec((None, 3, tn), lambda i, j: (layer, 0, off + j))

    def cb(off):
        return pl.BlockSpec((None, 1, tn), lambda i, j: (layer, 0, off + j))

    in_specs = [
        pl.BlockSpec((tm, d), lambda i, j: (i, 0), pipeline_mode=pl.Buffered(1)),
        pl.BlockSpec((HALO, d), lambda i, j: (jnp.maximum(i * (tm // HALO) - 1, 0), 0)),
        pl.BlockSpec((HALO, d), lambda i, j: (jnp.minimum((i + 1) * (tm // HALO), halo_blocks - 1), 0)),
        w(0), w(nj), cw(0), cw(nj), cb(0), cb(nj),
    ]
    cb3 = conv_b.reshape(conv_b.shape[0], 1, two_f)
    return pl.pallas_call(
        functools.partial(_ffn_front_body, seq_len=group.seq_len),
        grid=(rows // tm, nj),
        in_specs=in_specs,
        out_specs=pl.BlockSpec((tm, tn), lambda i, j: (i, j)),
        out_shape=jax.ShapeDtypeStruct((rows, d_ff), BF16),
        scratch_shapes=[pltpu.VMEM((tm + 2 * HALO, 2 * tn), F32)],
        compiler_params=_params(("parallel", "arbitrary")),
        name="ffn_front",
    )(h, h, h, w_up, w_up, conv_w, conv_w, cb3, cb3)


def kernel(x_prompt, x_sample, cache_a_k, cache_a_v, cache_b_k, cache_b_v, c, c_ctx,
           w_ada, b_ada, norm1_g, norm2_g,
           w_in_even, qn_a, kn_a, lam_q1, lam_k1, lam_q2, lam_k2, subln_g, qn_b, kn_b, rpb, w_out_even,
           w_in_odd, vnorm_g, w_sp, b_sp, w_out_odd,
           w_up, conv_w, conv_b, w_down):
    batch, seq, d = x_prompt.shape
    dec_batch, dec_seq, _ = x_sample.shape
    past_len = cache_a_k.shape[2]
    depth = w_ada.shape[0]
    n_even = w_in_even.shape[0]

    groups = (_Group(batch * seq, seq, 0), _Group(dec_batch * dec_seq, dec_seq, 1))
    xs = [x_prompt.reshape(batch * seq, d), x_sample.reshape(dec_batch * dec_seq, d)]

    cond8 = jnp.concatenate([c_ctx[None], c, jnp.zeros((8 - 1 - dec_batch, d), F32)])

    w_in_even_b, w_out_even_b, w_in_odd_b, w_out_odd_b, w_up_b = (
        w_in_even, w_out_even, w_in_odd, w_out_odd, w_up)
    w_down_b = w_down.astype(BF16)

    prompt_qkv = []
    for layer in range(depth):
        mod = _adaln(cond8, w_ada, b_ada, layer).reshape(8, 6, 1, d)
        if layer % 2 == 0:
            e = layer // 2
            lam_init = 0.8 - 0.6 * math.exp(-0.3 * layer)
            lam_vecs = jnp.stack([lam_q1[e], lam_k1[e], lam_q2[e], lam_k2[e]])
            ones = jnp.ones((SECTION,), F32)
            reps = SECTION // HEAD_DIM
            gains = jnp.concatenate([jnp.tile(qn_a[e] * Q_PRESCALE, reps), jnp.tile(kn_a[e], reps), ones,
                                     jnp.tile(qn_b[e] * Q_PRESCALE, reps), jnp.tile(kn_b[e], reps), ones])[None]
            sub_g = subln_g[e][None]
            mixed = []
            for gi, group in enumerate(groups):
                h = _norm_mod(xs[gi], norm1_g, layer, mod, 0, group)
                if gi == 0:
                    qkv = _qkv_proj(h, w_in_even_b, e, gains, group, rope=False, out_dtype=F32)
                    prompt_qkv.append(qkv)
                    oa, ob = _prompt_attn(qkv, lam_vecs, sub_g, lam_init, group.seq_len)
                else:
                    qkv = _qkv_proj(h, w_in_even_b, e, gains, group, rope=True, out_dtype=BF16)
                    ck_a = cache_a_k[:, e].reshape(dec_batch * past_len, SECTION)
                    cv_a = cache_a_v[:, e].reshape(dec_batch * past_len, SECTION)
                    ck_b = cache_b_k[:, e].reshape(dec_batch * past_len, SECTION)
                    cv_b = cache_b_v[:, e].reshape(dec_batch * past_len, SECTION)
                    oa = _latent_diff_attn(qkv, ck_a, cv_a, lam_vecs, sub_g, lam_init,
                                           group.seq_len, past_len)
                    ob = _latent_na_attn(qkv, ck_b, cv_b, rpb[e], group.seq_len, past_len)
                mixed.append((oa, ob))
            for gi, group in enumerate(groups):
                xs[gi] = _matmul_resid(mixed[gi], w_out_even_b, e, xs[gi], mod, 2, group,
                                       name="even_out_proj")
        else:
            o = layer // 2
            for gi, group in enumerate(groups):
                h = _norm_mod(xs[gi], norm1_g, layer, mod, 0, group)
                z = _matmul(h, w_in_odd_b, o, _gelu_epilogue, (), [], F32, name="odd_in_proj")
                gated = _spatial_gate(z, vnorm_g, w_sp, b_sp, o)
                xs[gi] = _matmul_resid(gated, w_out_odd_b, o, xs[gi], mod, 2, group,
                                       name="odd_out_proj")
        for gi, group in enumerate(groups):
            h = _norm_mod(xs[gi], norm2_g, layer, mod, 3, group)
            act = _ffn_front(h, w_up_b, conv_w, conv_b, layer, group)
            half = _matmul(act, w_down_b, layer, _plain_epilogue, (), [], F32,
                           k_parts=2, k_part=0, name="ffn_down_lo")
            xs[gi] = _matmul_resid(act, w_down_b, layer, xs[gi], mod, 5, group, partial_sum=half,
                                   k_parts=2, k_part=1, name="ffn_down_hi")

    def state(col0, shape):
        parts = [q[:, col0:col0 + SECTION].reshape((batch, seq) + shape) for q in prompt_qkv]
        return jnp.stack(parts, axis=1)

    h_a, h_b = N_HEADS_A, N_HEADS_B
    return (xs[0].reshape(batch, seq, d),
            xs[1].reshape(dec_batch, dec_seq, d),
            state(1 * SECTION, (2, h_a, HEAD_DIM)),
            state(2 * SECTION, (h_a, 2 * HEAD_DIM)),
            state(4 * SECTION, (h_b, HEAD_DIM)),
            state(5 * SECTION, (h_b, HEAD_DIM)))
```

```python
import functools
import math

import jax
import jax.numpy as jnp
from jax import lax
from jax.experimental import pallas as pl
from jax.experimental.pallas import tpu as pltpu

F32 = jnp.float32
BF16 = jnp.bfloat16

NORM_EPS = 1e-6
ROPE_BASE = 10000.0
GRID_W = 64
HEAD_DIM = 128
N_HEADS_A = 8
N_HEADS_B = 16
SECTION = 2048
NA_ROWS = 8
NA_COLS = 16
CHUNK = 128
N_GROUPS = 8
MASK_VALUE = -1e30
LOG2E = math.log2(math.e)
Q_PRESCALE = HEAD_DIM ** -0.5 * LOG2E
LANE = 128
VMEM_LIMIT = 56 * 1024 * 1024

_NT = (((1,), (1,)), ((), ()))


def _params(semantics):
    return pltpu.CompilerParams(dimension_semantics=semantics,
                                vmem_limit_bytes=VMEM_LIMIT)


def _rms(x):
    return x * lax.rsqrt(jnp.mean(x * x, axis=-1, keepdims=True) + NORM_EPS)


def _adaln_body(c_ref, w_ref, b_ref, o_ref):
    c = c_ref[...]
    s = c * jax.nn.sigmoid(c)
    o_ref[...] = jnp.dot(s.astype(BF16), w_ref[...].astype(BF16),
                         preferred_element_type=F32) + b_ref[...]


def _adaln(cond8, w_ada, b_ada, layer):
    _, d, n = w_ada.shape
    tn = 1024
    return pl.pallas_call(
        _adaln_body,
        grid=(n // tn,),
        in_specs=[pl.BlockSpec((8, d), lambda j: (0, 0)),
                  pl.BlockSpec((None, d, tn), lambda j: (layer, 0, j)),
                  pl.BlockSpec((None, 1, tn), lambda j: (layer, 0, j))],
        out_specs=pl.BlockSpec((8, tn), lambda j: (0, j)),
        out_shape=jax.ShapeDtypeStruct((8, n), F32),
        compiler_params=_params(("parallel",)),
        name="adaln",
    )(cond8, w_ada, b_ada.reshape(b_ada.shape[0], 1, n))


class _Group:
    def __init__(self, rows, seq_len, cond_base):
        self.rows = rows
        self.seq_len = seq_len
        self.cond_base = cond_base

    def cond(self, row0):
        if self.cond_base == 0:
            return 0
        return self.cond_base + row0 // self.seq_len


def _mod_spec(group, tm, tn, which, col_of):
    return pl.BlockSpec(
        (None, None, 1, tn),
        lambda *ids: (group.cond(ids[0] * tm), which, 0, col_of(*ids)))


NORM_TM = 512
NORM_SUB = 16


def _norm_mod_body(x_ref, g_ref, sh_ref, sc_ref, o_ref):
    gain, scale, shift = g_ref[...], 1 + sc_ref[...], sh_ref[...]
    for r in range(0, x_ref.shape[0], NORM_SUB):
        y = _rms(x_ref[r:r + NORM_SUB, :]) * gain
        o_ref[r:r + NORM_SUB, :] = (y * scale + shift).astype(o_ref.dtype)


def _norm_mod(x, gains, layer, mod, shift_idx, group):
    rows, d = x.shape
    tm = NORM_TM
    return pl.pallas_call(
        _norm_mod_body,
        grid=(rows // tm,),
        in_specs=[pl.BlockSpec((tm, d), lambda i: (i, 0)),
                  pl.BlockSpec((None, 1, d), lambda i: (layer, 0, 0)),
                  _mod_spec(group, tm, d, shift_idx, lambda i: 0),
                  _mod_spec(group, tm, d, shift_idx + 1, lambda i: 0)],
        out_specs=pl.BlockSpec((tm, d), lambda i: (i, 0)),
        out_shape=jax.ShapeDtypeStruct((rows, d), BF16),
        compiler_params=_params(("parallel",)),
        name="norm_mod",
    )(x, gains.reshape(gains.shape[0], 1, d), mod, mod)


MM_TM = 1024
MM_TN = 512
MM_CHUNK = 256


def _mm_body(*refs, n_x, n_extra, epilogue):
    x_refs, w_ref = refs[:n_x], refs[n_x]
    extras = refs[n_x + 1:n_x + 1 + n_extra]
    o_ref = refs[n_x + 1 + n_extra]
    j = pl.program_id(1)
    w = w_ref[...].astype(BF16)
    tm = x_refs[0].shape[0]
    sizes = (MM_CHUNK,) * (tm // MM_CHUNK - 1) + (MM_CHUNK // 2,) * 2
    for c in range(len(sizes)):
        rows = slice(sum(sizes[:c]), sum(sizes[:c + 1]))
        acc, k0 = None, 0
        for x_ref in x_refs:
            k1 = k0 + x_ref.shape[1]
            part = jnp.dot(x_ref[rows, :], w[k0:k1, :], preferred_element_type=F32)
            acc = part if acc is None else acc + part
            k0 = k1
        epilogue(acc, extras, o_ref, j, rows)


def _matmul(x, w, layer, epilogue, extras, extra_specs, out_dtype, *,
            k_parts=1, k_part=0, name):
    xs = x if isinstance(x, tuple) else (x,)
    assert len(xs) == 1 or k_parts == 1
    rows = xs[0].shape[0]
    kdim = sum(p.shape[1] for p in xs)
    n = w.shape[-1]
    tm, tn = MM_TM, MM_TN
    tk = kdim // k_parts
    body = functools.partial(_mm_body, n_x=len(xs), n_extra=len(extras), epilogue=epilogue)
    x_specs = [pl.BlockSpec((tm, p.shape[1] // k_parts), lambda i, j: (i, k_part)) for p in xs]
    return pl.pallas_call(
        body,
        grid=(rows // tm, n // tn),
        in_specs=[*x_specs,
                  pl.BlockSpec((None, tk, tn), lambda i, j: (layer, k_part, j)),
                  *extra_specs],
        out_specs=pl.BlockSpec((tm, tn), lambda i, j: (i, j)),
        out_shape=jax.ShapeDtypeStruct((rows, n), out_dtype),
        compiler_params=_params(("parallel", "arbitrary")),
        name=name,
    )(*xs, w, *extras)


def _plain_epilogue(acc, extras, o_ref, j, rows):
    o_ref[rows, :] = acc.astype(o_ref.dtype)


def _gelu_epilogue(acc, extras, o_ref, j, rows):
    o_ref[rows, :] = (0.5 * acc * (1 + lax.erf(acc * math.sqrt(0.5)))).astype(o_ref.dtype)


def _resid_epilogue(acc, extras, o_ref, j, rows):
    x_ref, gate_ref = extras[:2]
    if len(extras) == 3:
        acc = extras[2][rows, :] + acc
    o_ref[rows, :] = x_ref[rows, :] + gate_ref[...] * acc


def _matmul_resid(h, w, layer, x, mod, gate_idx, group, *, partial_sum=None,
                  k_parts=1, k_part=0, name):
    tile = pl.BlockSpec((MM_TM, MM_TN), lambda i, j: (i, j))
    specs = [tile, _mod_spec(group, MM_TM, MM_TN, gate_idx, lambda i, j: j)]
    extras = (x, mod)
    if partial_sum is not None:
        specs.append(tile)
        extras += (partial_sum,)
    return _matmul(h, w, layer, _resid_epilogue, extras, specs, F32,
                   k_parts=k_parts, k_part=k_part, name=name)


def _qkv_epilogue(acc, extras, o_ref, j, rows, *, rope):
    g_ref = extras[0]
    sec = (j * MM_TN) // SECTION
    is_v = jnp.logical_or(sec == 2, sec == 5)
    if rope:
        is_rope = sec < 2
        cos = jnp.where(is_rope, extras[1][rows, :], 1.0)
        sin_lo = jnp.where(is_rope, extras[2][rows, :], 0.0)
        sin_hi = jnp.where(is_rope, extras[3][rows, :], 0.0)
    for c in range(MM_TN // LANE):
        cols = slice(c * LANE, (c + 1) * LANE)
        a = acc[:, cols]
        y = jnp.where(is_v, a, _rms(a) * g_ref[:, cols])
        if rope:
            y = (y * cos + pltpu.roll(y, LANE - 32, 1) * sin_lo
                 + pltpu.roll(y, 32, 1) * sin_hi)
        o_ref[rows, cols] = y.astype(o_ref.dtype)


def _rope_tables(n_tok):
    quarter = HEAD_DIM // 4
    t = jnp.arange(n_tok)
    inv = ROPE_BASE ** (-jnp.arange(quarter, dtype=F32) / quarter)

    def ang(p):
        a = p.astype(F32)[:, None] * inv[None, :]
        return jnp.concatenate([a, a], axis=-1)

    angles = jnp.concatenate([ang(t // GRID_W), ang(t % GRID_W)], axis=-1)
    cos, sin = jnp.cos(angles), jnp.sin(angles)
    low = (jnp.arange(HEAD_DIM) % (2 * quarter)) < quarter
    return cos, jnp.where(low, -sin, 0.0), jnp.where(low, 0.0, sin)


def _qkv_proj(h, w, layer, gains, group, *, rope, out_dtype):
    extras = [gains]
    specs = [pl.BlockSpec((1, MM_TN), lambda i, j: (0, j))]
    if rope:
        tiles_per_seq = group.seq_len // MM_TM
        tab = pl.BlockSpec((MM_TM, HEAD_DIM), lambda i, j: (i % tiles_per_seq, 0))
        extras += list(_rope_tables(group.seq_len))
        specs += [tab, tab, tab]
    epi = functools.partial(_qkv_epilogue, rope=rope)
    return _matmul(h, w, layer, epi, tuple(extras), specs, out_dtype, name="qkv_proj")


def _lam_value(lam_ref, lam_init):
    v = lam_ref[...]
    t1 = jnp.sum(v[0:1] * v[1:2], axis=-1, keepdims=True)
    t2 = jnp.sum(v[2:3] * v[3:4], axis=-1, keepdims=True)
    return jnp.exp(t1) - jnp.exp(t2) + lam_init


def _scores(q, k):
    return lax.dot_general(q.astype(BF16), k.astype(BF16), _NT,
                           preferred_element_type=F32)


def _softmax(s):
    e = jnp.exp2(s - jnp.max(s, axis=-1, keepdims=True))
    return e / jnp.sum(e, axis=-1, keepdims=True)


def _joint_exp(s_a, s_b):
    m = jnp.maximum(jnp.max(s_a, axis=-1, keepdims=True),
                    jnp.max(s_b, axis=-1, keepdims=True))
    e_a = jnp.exp2(s_a - m)
    e_b = jnp.exp2(s_b - m)
    denom = (jnp.sum(e_a, axis=-1, keepdims=True)
             + jnp.sum(e_b, axis=-1, keepdims=True))
    return e_a.astype(BF16), e_b.astype(BF16), denom


def _subln(o, g_ref, lam_init):
    return _rms(o) * g_ref[...] * (1.0 - lam_init)


def _prompt_attn_body(lam_ref, q0, q1, k0, k1, va, qb, kb, vb, g_ref,
                      oa_ref, ob_ref, *, lam_init):
    lam = _lam_value(lam_ref, lam_init)
    a = _softmax(_scores(q0[...], k0[...])) - lam * _softmax(_scores(q1[...], k1[...]))
    o = jnp.dot(a.astype(BF16), va[...].astype(BF16), preferred_element_type=F32)
    oa_ref[...] = _subln(o, g_ref, lam_init).astype(oa_ref.dtype)
    for t in range(2):
        cols = slice(t * HEAD_DIM, (t + 1) * HEAD_DIM)
        p = _softmax(_scores(qb[:, cols], kb[:, cols]))
        ob_ref[:, cols] = jnp.dot(p.astype(BF16), vb[:, cols].astype(BF16),
                                  preferred_element_type=F32).astype(ob_ref.dtype)


def _prompt_attn(qkv, lam_vecs, subln_g, lam_init, seq_len):
    rows = qkv.shape[0]
    nb = rows // seq_len
    sec128 = SECTION // HEAD_DIM
    sec256 = SECTION // (2 * HEAD_DIM)

    def blk(width, col0):
        return pl.BlockSpec((seq_len, width), lambda b, h: (b, col0 + h))

    half = sec128 // 2
    in_specs = [
        pl.BlockSpec((4, HEAD_DIM), lambda b, h: (0, 0)),
        blk(HEAD_DIM, 0), blk(HEAD_DIM, half),
        blk(HEAD_DIM, sec128), blk(HEAD_DIM, sec128 + half),
        blk(2 * HEAD_DIM, 2 * sec256),
        blk(2 * HEAD_DIM, 3 * sec256),
        blk(2 * HEAD_DIM, 4 * sec256),
        blk(2 * HEAD_DIM, 5 * sec256),
        pl.BlockSpec((1, 2 * HEAD_DIM), lambda b, h: (0, 0)),
    ]
    out_spec = pl.BlockSpec((seq_len, 2 * HEAD_DIM), lambda b, h: (b, h))
    out = jax.ShapeDtypeStruct((rows, SECTION), BF16)
    return pl.pallas_call(
        functools.partial(_prompt_attn_body, lam_init=lam_init),
        grid=(nb, N_HEADS_A),
        in_specs=in_specs,
        out_specs=[out_spec, out_spec],
        out_shape=[out, out],
        compiler_params=_params(("parallel", "parallel")),
        name="prompt_attn",
    )(lam_vecs, qkv, qkv, qkv, qkv, qkv, qkv, qkv, qkv, subln_g)


DIFF_KEY_CHUNK = 1024


def _latent_diff_body(lam_ref, q0, q1, k0, k1, v, kc0, kc1, vc, g_ref, o_ref,
                      *, lam_init):
    lam = _lam_value(lam_ref, lam_init)
    v_ctx = vc[...].astype(BF16)
    n_lat = k0.shape[0]

    def attend(q, k, kc):
        qv = q[...]
        s = _scores(qv, kc[...])
        m = jnp.max(s, axis=-1, keepdims=True)
        e = jnp.exp2(s - m)
        denom = jnp.sum(e, axis=-1, keepdims=True)
        acc = jnp.dot(e.astype(BF16), v_ctx, preferred_element_type=F32)
        for c in range(n_lat // DIFF_KEY_CHUNK):
            keys = slice(c * DIFF_KEY_CHUNK, (c + 1) * DIFF_KEY_CHUNK)
            s = _scores(qv, k[keys, :])
            m_new = jnp.maximum(m, jnp.max(s, axis=-1, keepdims=True))
            alpha = jnp.exp2(m - m_new)
            e = jnp.exp2(s - m_new)
            denom = denom * alpha + jnp.sum(e, axis=-1, keepdims=True)
            acc = acc * alpha + jnp.dot(e.astype(BF16), v[keys, :], preferred_element_type=F32)
            m = m_new
        return acc / denom

    o = attend(q0, k0, kc0) - lam * attend(q1, k1, kc1)
    o_ref[...] = _subln(o, g_ref, lam_init).astype(o_ref.dtype)


def _latent_diff_attn(qkv, cache_k, cache_v, lam_vecs, subln_g, lam_init,
                      seq_len, ctx_len):
    rows = qkv.shape[0]
    nb = rows // seq_len
    tq = 512
    nq = seq_len // tq
    sec128 = SECTION // HEAD_DIM
    sec256 = SECTION // (2 * HEAD_DIM)
    half = sec128 // 2

    def qblk(col0):
        return pl.BlockSpec((tq, HEAD_DIM), lambda b, h, i: (b * nq + i, col0 + h))

    def kvblk(n, width, col0):
        return pl.BlockSpec((n, width), lambda b, h, i: (b, col0 + h))

    in_specs = [
        pl.BlockSpec((4, HEAD_DIM), lambda b, h, i: (0, 0)),
        qblk(0), qblk(half),
        kvblk(seq_len, HEAD_DIM, sec128), kvblk(seq_len, HEAD_DIM, sec128 + half),
        kvblk(seq_len, 2 * HEAD_DIM, 2 * sec256),
        kvblk(ctx_len, HEAD_DIM, 0), kvblk(ctx_len, HEAD_DIM, half),
        kvblk(ctx_len, 2 * HEAD_DIM, 0),
        pl.BlockSpec((1, 2 * HEAD_DIM), lambda b, h, i: (0, 0)),
    ]
    return pl.pallas_call(
        functools.partial(_latent_diff_body, lam_init=lam_init),
        grid=(nb, N_HEADS_A, nq),
        in_specs=in_specs,
        out_specs=pl.BlockSpec((tq, 2 * HEAD_DIM), lambda b, h, i: (b * nq + i, h)),
        out_shape=jax.ShapeDtypeStruct((rows, SECTION), BF16),
        compiler_params=_params(("parallel", "parallel", "arbitrary")),
        name="latent_diff_attn",
    )(lam_vecs, qkv, qkv, qkv, qkv, qkv, cache_k, cache_k, cache_v, subln_g)


NA_Q_ROWS = 8
NA_K_ROWS = 16


def _na_band_start(i, grid_rows):
    return jnp.clip(i * NA_Q_ROWS - NA_ROWS // 2, 0, grid_rows - NA_K_ROWS)


NA_D_MIN = -NA_Q_ROWS
NA_D_MAX = NA_K_ROWS + NA_ROWS - 2


def _na_pair_tables(rpb):
    n_h, n_dr, n_dc = rpb.shape
    u = jnp.concatenate([rpb[..., NA_COLS - 1:],
                         jnp.zeros((n_h, n_dr, LANE - n_dc), F32),
                         rpb[..., :NA_COLS - 1]], axis=-1)
    skew = jnp.tile(u, (1, 1, GRID_W))[..., :GRID_W * (LANE - 1)]
    toep = skew.reshape(n_h, n_dr, GRID_W, LANE - 1)[..., :GRID_W]
    cols = jnp.arange(GRID_W)
    cstart = jnp.clip(cols - NA_COLS // 2, 0, GRID_W - NA_COLS)
    col_ok = (cols[None, :] >= cstart[:, None]) & (cols[None, :] < cstart[:, None] + NA_COLS)
    toep = jnp.where(col_ok, toep, MASK_VALUE)
    ext = jnp.pad(toep, ((0, 0), (-NA_D_MIN, NA_D_MAX + 1 - n_dr), (0, 0), (0, 0)),
                  constant_values=MASK_VALUE)
    return jnp.concatenate([ext[:, :-1], ext[:, 1:]], axis=-1)


def _latent_na_body(q, k, v, kc, vc, tab_ref, o_ref, bias_ref, *, grid_rows):
    i = pl.program_id(1)
    n_blocks = grid_rows // NA_Q_ROWS
    row0 = i * NA_Q_ROWS
    band0 = _na_band_start(i, grid_rows)

    @pl.when(jnp.logical_or(i <= 1, i == n_blocks - 1))
    def _():
        lane = lax.broadcasted_iota(jnp.int32, (GRID_W, LANE), 1)
        for rq in range(NA_Q_ROWS):
            r = row0 + rq
            rs = jnp.clip(r - NA_ROWS // 2, 0, grid_rows - NA_ROWS)

            def penalty(kr):
                ok = jnp.logical_and(kr >= rs, kr < rs + NA_ROWS)
                return jnp.where(ok, 0.0, MASK_VALUE)

            for p in range(NA_K_ROWS // 2):
                kr = band0 + 2 * p
                d = kr - r + NA_ROWS - 1
                pen = jnp.where(lane < GRID_W, penalty(kr), penalty(kr + 1))
                bias_ref[rq * GRID_W:(rq + 1) * GRID_W, p * LANE:(p + 1) * LANE] = (
                    tab_ref[d - NA_D_MIN] * LOG2E + pen)

    start = pl.multiple_of(band0 * GRID_W, GRID_W)
    band = pl.ds(start, NA_K_ROWS * GRID_W)
    half = q.shape[1] // 2
    for b in range(q.shape[0]):
        k_band, v_band = k[b, band, :], v[b, band, :]
        k_ctx, v_ctx = kc[b], vc[b].astype(BF16)
        for part in range(2):
            rows = slice(part * half, (part + 1) * half)
            e_loc, e_ctx, denom = _joint_exp(_scores(q[b, rows, :], k_band) + bias_ref[rows, :],
                                             _scores(q[b, rows, :], k_ctx))
            o = (jnp.dot(e_loc, v_band, preferred_element_type=F32)
                 + jnp.dot(e_ctx, v_ctx, preferred_element_type=F32))
            o_ref[b, rows, :] = (o / denom).astype(o_ref.dtype)


def _latent_na_attn(qkv, cache_k, cache_v, rpb, seq_len, ctx_len):
    rows = qkv.shape[0]
    nb = rows // seq_len
    grid_rows = seq_len // GRID_W
    n_blocks = grid_rows // NA_Q_ROWS
    tq = NA_Q_ROWS * GRID_W
    sec128 = SECTION // HEAD_DIM
    tables = _na_pair_tables(rpb)

    qkv3 = qkv.reshape(nb, seq_len, qkv.shape[1])
    ck3 = cache_k.reshape(nb, ctx_len, SECTION)
    cv3 = cache_v.reshape(nb, ctx_len, SECTION)

    def kvblk(n, col0):
        return pl.BlockSpec((nb, n, HEAD_DIM), lambda h, i: (0, 0, col0 + h))

    in_specs = [
        pl.BlockSpec((nb, tq, HEAD_DIM), lambda h, i: (0, i, 3 * sec128 + h)),
        kvblk(seq_len, 4 * sec128), kvblk(seq_len, 5 * sec128),
        kvblk(ctx_len, 0), kvblk(ctx_len, 0),
        pl.BlockSpec((None,) + tables.shape[1:], lambda h, i: (h, 0, 0, 0)),
    ]
    out = pl.pallas_call(
        functools.partial(_latent_na_body, grid_rows=grid_rows),
        grid=(N_HEADS_B, n_blocks),
        in_specs=in_specs,
        out_specs=pl.BlockSpec((nb, tq, HEAD_DIM), lambda h, i: (0, i, h)),
        out_shape=jax.ShapeDtypeStruct((nb, seq_len, SECTION), BF16),
        scratch_shapes=[pltpu.VMEM((tq, NA_K_ROWS * GRID_W), F32)],
        compiler_params=_params(("arbitrary", "arbitrary")),
        name="latent_na_attn",
    )(qkv3, qkv3, qkv3, ck3, cv3, tables)
    return out.reshape(rows, SECTION)


def _spatial_gate_body(u_ref, v_ref, g_ref, ws_ref, bs_ref, o_ref):
    vn = (_rms(v_ref[...]) * g_ref[...]).astype(BF16)
    width = vn.shape[1] // N_GROUPS
    for g in range(N_GROUPS):
        cols = slice(g * width, (g + 1) * width)
        sv = jnp.dot(ws_ref[g].astype(BF16), vn[:, cols],
                     preferred_element_type=F32) + bs_ref[:, g:g + 1]
        o_ref[:, cols] = (u_ref[:, cols] * sv).astype(o_ref.dtype)


def _spatial_gate(z, vnorm_g, w_sp, b_sp, layer):
    rows, two_d = z.shape
    d = two_d // 2
    return pl.pallas_call(
        _spatial_gate_body,
        grid=(rows // CHUNK,),
        in_specs=[pl.BlockSpec((CHUNK, d), lambda n: (n, 0)),
                  pl.BlockSpec((CHUNK, d), lambda n: (n, 1)),
                  pl.BlockSpec((None, 1, d), lambda n: (layer, 0, 0)),
                  pl.BlockSpec((None, N_GROUPS, CHUNK, CHUNK), lambda n: (layer, 0, 0, 0)),
                  pl.BlockSpec((None, CHUNK, N_GROUPS), lambda n: (layer, 0, 0))],
        out_specs=pl.BlockSpec((CHUNK, d), lambda n: (n, 0)),
        out_shape=jax.ShapeDtypeStruct((rows, d), BF16),
        compiler_params=_params(("parallel",)),
        name="spatial_gate",
    )(z, z, vnorm_g.reshape(vnorm_g.shape[0], 1, d), w_sp, jnp.swapaxes(b_sp, 1, 2))


FFN_TN = 256
HALO = 32
SUB = 8
FFN_TM = 2048
FFN_CHUNK = 256


def _ffn_chunks(tm):
    return (FFN_CHUNK,) * (tm // FFN_CHUNK - 1) + (FFN_CHUNK // 2,) * 2


def _ffn_front_body(x_ref, xp_ref, xn_ref, wg_ref, wu_ref, cwg_ref, cwu_ref,
                    cbg_ref, cbu_ref, o_ref, acc_ref, *, seq_len):
    tm = x_ref.shape[0]
    i = pl.program_id(0)
    sizes = _ffn_chunks(tm)
    starts = [sum(sizes[:c]) for c in range(len(sizes) + 1)]
    n_chunks = len(sizes)
    tn = wg_ref.shape[1]
    w = jnp.concatenate([wg_ref[...].astype(BF16), wu_ref[...].astype(BF16)], axis=1)

    def project(c):
        x = x_ref[starts[c]:starts[c + 1], :]
        lo, hi = HALO + starts[c], HALO + starts[c + 1]
        if c == 0:
            x, lo = jnp.concatenate([xp_ref[...], x], axis=0), 0
        if c == n_chunks - 1:
            x, hi = jnp.concatenate([x, xn_ref[...]], axis=0), tm + 2 * HALO
        acc_ref[lo:hi, :] = jnp.dot(x, w, preferred_element_type=F32)

    def conv(col0, cw_ref, cb_ref, c, at_start, at_end):
        n = sizes[c]
        r0 = HALO + starts[c]
        cols = slice(col0, col0 + tn)
        x_prev = acc_ref[r0 - 1:r0 - 1 + n, cols]
        x_next = acc_ref[r0 + 1:r0 + 1 + n, cols]
        row = lax.broadcasted_iota(jnp.int32, (SUB, 1), 0)
        head = jnp.where(jnp.logical_and(at_start, row == 0), 0.0, x_prev[:SUB])
        tail = jnp.where(jnp.logical_and(at_end, row == SUB - 1), 0.0, x_next[n - SUB:])
        x_prev = jnp.concatenate([head, x_prev[SUB:]], axis=0)
        x_next = jnp.concatenate([x_next[:n - SUB], tail], axis=0)
        return (cw_ref[0:1, :] * x_prev + cw_ref[1:2, :] * acc_ref[r0:r0 + n, cols]
                + cw_ref[2:3, :] * x_next + cb_ref[...])

    def gate(c):
        first = i * tm + starts[c]
        at_start = first % seq_len == 0
        at_end = (first + sizes[c]) % seq_len == 0
        g = conv(0, cwg_ref, cbg_ref, c, at_start, at_end)
        u = conv(tn, cwu_ref, cbu_ref, c, at_start, at_end)
        o_ref[starts[c]:starts[c + 1], :] = (g * jax.nn.sigmoid(g) * u).astype(o_ref.dtype)

    project(0)
    for c in range(n_chunks):
        if c + 1 < n_chunks:
            project(c + 1)
        gate(c)


def _ffn_front(h, w_up, conv_w, conv_b, layer, group):
    rows, d = h.shape
    two_f = w_up.shape[-1]
    d_ff = two_f // 2
    tm, tn = min(FFN_TM, rows), FFN_TN
    nj = d_ff // tn
    assert rows % tm == 0 and tm % FFN_CHUNK == 0 and d_ff % tn == 0
    assert group.seq_len % FFN_CHUNK == 0 and (tm % group.seq_len == 0 or group.seq_len % tm == 0)
    halo_blocks = rows // HALO

    def w(off):
        return pl.BlockSpec((None, d, tn), lambda i, j: (layer, 0, off + j))

    def cw(off):
        return pl.BlockSpec((None, 3, tn), lambda i, j: (layer, 0, off + j))

    def cb(off):
        return pl.BlockSpec((None, 1, tn), lambda i, j: (layer, 0, off + j))

    in_specs = [
        pl.BlockSpec((tm, d), lambda i, j: (i, 0), pipeline_mode=pl.Buffered(1)),
        pl.BlockSpec((HALO, d), lambda i, j: (jnp.maximum(i * (tm // HALO) - 1, 0), 0)),
        pl.BlockSpec((HALO, d), lambda i, j: (jnp.minimum((i + 1) * (tm // HALO), halo_blocks - 1), 0)),
        w(0), w(nj), cw(0), cw(nj), cb(0), cb(nj),
    ]
    cb3 = conv_b.reshape(conv_b.shape[0], 1, two_f)
    return pl.pallas_call(
        functools.partial(_ffn_front_body, seq_len=group.seq_len),
        grid=(rows // tm, nj),
        in_specs=in_specs,
        out_specs=pl.BlockSpec((tm, tn), lambda i, j: (i, j)),
        out_shape=jax.ShapeDtypeStruct((rows, d_ff), BF16),
        scratch_shapes=[pltpu.VMEM((tm + 2 * HALO, 2 * tn), F32)],
        compiler_params=_params(("parallel", "arbitrary")),
        name="ffn_front",
    )(h, h, h, w_up, w_up, conv_w, conv_w, cb3, cb3)


def kernel(x_prompt, x_sample, cache_a_k, cache_a_v, cache_b_k, cache_b_v, c, c_ctx,
           w_ada, b_ada, norm1_g, norm2_g,
           w_in_even, qn_a, kn_a, lam_q1, lam_k1, lam_q2, lam_k2, subln_g, qn_b, kn_b, rpb, w_out_even,
           w_in_odd, vnorm_g, w_sp, b_sp, w_out_odd,
           w_up, conv_w, conv_b, w_down):
    batch, seq, d = x_prompt.shape
    dec_batch, dec_seq, _ = x_sample.shape
    past_len = cache_a_k.shape[2]
    depth = w_ada.shape[0]
    n_even = w_in_even.shape[0]

    groups = (_Group(batch * seq, seq, 0), _Group(dec_batch * dec_seq, dec_seq, 1))
    xs = [x_prompt.reshape(batch * seq, d), x_sample.reshape(dec_batch * dec_seq, d)]

    cond8 = jnp.concatenate([c_ctx[None], c, jnp.zeros((8 - 1 - dec_batch, d), F32)])

    w_in_even_b, w_out_even_b, w_in_odd_b, w_out_odd_b, w_up_b = (
        w_in_even, w_out_even, w_in_odd, w_out_odd, w_up)
    w_down_b = w_down.astype(BF16)

    prompt_qkv = []
    for layer in range(depth):
        mod = _adaln(cond8, w_ada, b_ada, layer).reshape(8, 6, 1, d)
        if layer % 2 == 0:
            e = layer // 2
            lam_init = 0.8 - 0.6 * math.exp(-0.3 * layer)
            lam_vecs = jnp.stack([lam_q1[e], lam_k1[e], lam_q2[e], lam_k2[e]])
            ones = jnp.ones((SECTION,), F32)
            reps = SECTION // HEAD_DIM
            gains = jnp.concatenate([jnp.tile(qn_a[e] * Q_PRESCALE, reps), jnp.tile(kn_a[e], reps), ones,
                                     jnp.tile(qn_b[e] * Q_PRESCALE, reps), jnp.tile(kn_b[e], reps), ones])[None]
            sub_g = subln_g[e][None]
            mixed = []
            for gi, group in enumerate(groups):
                h = _norm_mod(xs[gi], norm1_g, layer, mod, 0, group)
                if gi == 0:
                    qkv = _qkv_proj(h, w_in_even_b, e, gains, group, rope=False, out_dtype=F32)
                    prompt_qkv.append(qkv)
                    oa, ob = _prompt_attn(qkv, lam_vecs, sub_g, lam_init, group.seq_len)
                else:
                    qkv = _qkv_proj(h, w_in_even_b, e, gains, group, rope=True, out_dtype=BF16)
                    ck_a = cache_a_k[:, e].reshape(dec_batch * past_len, SECTION)
                    cv_a = cache_a_v[:, e].reshape(dec_batch * past_len, SECTION)
                    ck_b = cache_b_k[:, e].reshape(dec_batch * past_len, SECTION)
                    cv_b = cache_b_v[:, e].reshape(dec_batch * past_len, SECTION)
                    oa = _latent_diff_attn(qkv, ck_a, cv_a, lam_vecs, sub_g, lam_init,
                                           group.seq_len, past_len)
                    ob = _latent_na_attn(qkv, ck_b, cv_b, rpb[e], group.seq_len, past_len)
                mixed.append((oa, ob))
            for gi, group in enumerate(groups):
                xs[gi] = _matmul_resid(mixed[gi], w_out_even_b, e, xs[gi], mod, 2, group,
                                       name="even_out_proj")
        else:
            o = layer // 2
            for gi, group in enumerate(groups):
                h = _norm_mod(xs[gi], norm1_g, layer, mod, 0, group)
                z = _matmul(h, w_in_odd_b, o, _gelu_epilogue, (), [], F32, name="odd_in_proj")
                gated = _spatial_gate(z, vnorm_g, w_sp, b_sp, o)
                xs[gi] = _matmul_resid(gated, w_out_odd_b, o, xs[gi], mod, 2, group,
                                       name="odd_out_proj")
        for gi, group in enumerate(groups):
            h = _norm_mod(xs[gi], norm2_g, layer, mod, 3, group)
            act = _ffn_front(h, w_up_b, conv_w, conv_b, layer, group)
            half = _matmul(act, w_down_b, layer, _plain_epilogue, (), [], F32,
                           k_parts=2, k_part=0, name="ffn_down_lo")
            xs[gi] = _matmul_resid(act, w_down_b, layer, xs[gi], mod, 5, group, partial_sum=half,
                                   k_parts=2, k_part=1, name="ffn_down_hi")

    def state(col0, shape):
        parts = [q[:, col0:col0 + SECTION].reshape((batch, seq) + shape) for q in prompt_qkv]
        return jnp.stack(parts, axis=1)

    h_a, h_b = N_HEADS_A, N_HEADS_B
    return (xs[0].reshape(batch, seq, d),
            xs[1].reshape(dec_batch, dec_seq, d),
            state(1 * SECTION, (2, h_a, HEAD_DIM)),
            state(2 * SECTION, (h_a, 2 * HEAD_DIM)),
            state(4 * SECTION, (h_b, HEAD_DIM)),
            state(5 * SECTION, (h_b, HEAD_DIM)))
```

```python
import functools
import math

import jax
import jax.numpy as jnp
from jax import lax
from jax.experimental import pallas as pl
from jax.experimental.pallas import tpu as pltpu

F32 = jnp.float32
BF16 = jnp.bfloat16

NORM_EPS = 1e-6
ROPE_BASE = 10000.0
GRID_W = 64
HEAD_DIM = 128
N_HEADS_A = 8
N_HEADS_B = 16
SECTION = 2048
NA_ROWS = 8
NA_COLS = 16
CHUNK = 128
N_GROUPS = 8
MASK_VALUE = -1e30
LOG2E = math.log2(math.e)
Q_PRESCALE = HEAD_DIM ** -0.5 * LOG2E
LANE = 128
VMEM_LIMIT = 56 * 1024 * 1024

_NT = (((1,), (1,)), ((), ()))


def _params(semantics):
    return pltpu.CompilerParams(dimension_semantics=semantics,
                                vmem_limit_bytes=VMEM_LIMIT)


def _rms(x):
    return x * lax.rsqrt(jnp.mean(x * x, axis=-1, keepdims=True) + NORM_EPS)


def _adaln_body(c_ref, w_ref, b_ref, o_ref):
    c = c_ref[...]
    s = c * jax.nn.sigmoid(c)
    o_ref[...] = jnp.dot(s.astype(BF16), w_ref[...].astype(BF16),
                         preferred_element_type=F32) + b_ref[...]


def _adaln(cond8, w_ada, b_ada, layer):
    _, d, n = w_ada.shape
    tn = 1024
    return pl.pallas_call(
        _adaln_body,
        grid=(n // tn,),
        in_specs=[pl.BlockSpec((8, d), lambda j: (0, 0)),
                  pl.BlockSpec((None, d, tn), lambda j: (layer, 0, j)),
                  pl.BlockSpec((None, 1, tn), lambda j: (layer, 0, j))],
        out_specs=pl.BlockSpec((8, tn), lambda j: (0, j)),
        out_shape=jax.ShapeDtypeStruct((8, n), F32),
        compiler_params=_params(("parallel",)),
        name="adaln",
    )(cond8, w_ada, b_ada.reshape(b_ada.shape[0], 1, n))


class _Group:
    def __init__(self, rows, seq_len, cond_base):
        self.rows = rows
        self.seq_len = seq_len
        self.cond_base = cond_base

    def cond(self, row0):
        if self.cond_base == 0:
            return 0
        return self.cond_base + row0 // self.seq_len


def _mod_spec(group, tm, tn, which, col_of):
    return pl.BlockSpec(
        (None, None, 1, tn),
        lambda *ids: (group.cond(ids[0] * tm), which, 0, col_of(*ids)))


NORM_TM = 512
NORM_SUB = 16


def _norm_mod_body(x_ref, g_ref, sh_ref, sc_ref, o_ref):
    gain, scale, shift = g_ref[...], 1 + sc_ref[...], sh_ref[...]
    for r in range(0, x_ref.shape[0], NORM_SUB):
        y = _rms(x_ref[r:r + NORM_SUB, :]) * gain
        o_ref[r:r + NORM_SUB, :] = (y * scale + shift).astype(o_ref.dtype)


def _norm_mod(x, gains, layer, mod, shift_idx, group):
    rows, d = x.shape
    tm = NORM_TM
    return pl.pallas_call(
        _norm_mod_body,
        grid=(rows // tm,),
        in_specs=[pl.BlockSpec((tm, d), lambda i: (i, 0)),
                  pl.BlockSpec((None, 1, d), lambda i: (layer, 0, 0)),
                  _mod_spec(group, tm, d, shift_idx, lambda i: 0),
                  _mod_spec(group, tm, d, shift_idx + 1, lambda i: 0)],
        out_specs=pl.BlockSpec((tm, d), lambda i: (i, 0)),
        out_shape=jax.ShapeDtypeStruct((rows, d), BF16),
        compiler_params=_params(("parallel",)),
        name="norm_mod",
    )(x, gains.reshape(gains.shape[0], 1, d), mod, mod)


MM_TM = 1024
MM_TN = 512
MM_CHUNK = 256


def _mm_body(*refs, n_x, n_extra, epilogue):
    x_refs, w_ref = refs[:n_x], refs[n_x]
    extras = refs[n_x + 1:n_x + 1 + n_extra]
    o_ref = refs[n_x + 1 + n_extra]
    j = pl.program_id(1)
    w = w_ref[...].astype(BF16)
    tm = x_refs[0].shape[0]
    sizes = (MM_CHUNK,) * (tm // MM_CHUNK - 1) + (MM_CHUNK // 2,) * 2
    for c in range(len(sizes)):
        rows = slice(sum(sizes[:c]), sum(sizes[:c + 1]))
        acc, k0 = None, 0
        for x_ref in x_refs:
            k1 = k0 + x_ref.shape[1]
            part = jnp.dot(x_ref[rows, :], w[k0:k1, :], preferred_element_type=F32)
            acc = part if acc is None else acc + part
            k0 = k1
        epilogue(acc, extras, o_ref, j, rows)


def _matmul(x, w, layer, epilogue, extras, extra_specs, out_dtype, *,
            k_parts=1, k_part=0, name):
    xs = x if isinstance(x, tuple) else (x,)
    assert len(xs) == 1 or k_parts == 1
    rows = xs[0].shape[0]
    kdim = sum(p.shape[1] for p in xs)
    n = w.shape[-1]
    tm, tn = MM_TM, MM_TN
    tk = kdim // k_parts
    body = functools.partial(_mm_body, n_x=len(xs), n_extra=len(extras), epilogue=epilogue)
    x_specs = [pl.BlockSpec((tm, p.shape[1] // k_parts), lambda i, j: (i, k_part)) for p in xs]
    return pl.pallas_call(
        body,
        grid=(rows // tm, n // tn),
        in_specs=[*x_specs,
                  pl.BlockSpec((None, tk, tn), lambda i, j: (layer, k_part, j)),
                  *extra_specs],
        out_specs=pl.BlockSpec((tm, tn), lambda i, j: (i, j)),
        out_shape=jax.ShapeDtypeStruct((rows, n), out_dtype),
        compiler_params=_params(("parallel", "arbitrary")),
        name=name,
    )(*xs, w, *extras)


def _plain_epilogue(acc, extras, o_ref, j, rows):
    o_ref[rows, :] = acc.astype(o_ref.dtype)


def _gelu_epilogue(acc, extras, o_ref, j, rows):
    o_ref[rows, :] = (0.5 * acc * (1 + lax.erf(acc * math.sqrt(0.5)))).astype(o_ref.dtype)


def _resid_epilogue(acc, extras, o_ref, j, rows):
    x_ref, gate_ref = extras[:2]
    if len(extras) == 3:
        acc = extras[2][rows, :] + acc
    o_ref[rows, :] = x_ref[rows, :] + gate_ref[...] * acc


def _matmul_resid(h, w, layer, x, mod, gate_idx, group, *, partial_sum=None,
                  k_parts=1, k_part=0, name):
    tile = pl.BlockSpec((MM_TM, MM_TN), lambda i, j: (i, j))
    specs = [tile, _mod_spec(group, MM_TM, MM_TN, gate_idx, lambda i, j: j)]
    extras = (x, mod)
    if partial_sum is not None:
        specs.append(tile)
        extras += (partial_sum,)
    return _matmul(h, w, layer, _resid_epilogue, extras, specs, F32,
                   k_parts=k_parts, k_part=k_part, name=name)


def _qkv_epilogue(acc, extras, o_ref, j, rows, *, rope):
    g_ref = extras[0]
    sec = (j * MM_TN) // SECTION
    is_v = jnp.logical_or(sec == 2, sec == 5)
    if rope:
        is_rope = sec < 2
        cos = jnp.where(is_rope, extras[1][rows, :], 1.0)
        sin_lo = jnp.where(is_rope, extras[2][rows, :], 0.0)
        sin_hi = jnp.where(is_rope, extras[3][rows, :], 0.0)
    for c in range(MM_TN // LANE):
        cols = slice(c * LANE, (c + 1) * LANE)
        a = acc[:, cols]
        y = jnp.where(is_v, a, _rms(a) * g_ref[:, cols])
        if rope:
            y = (y * cos + pltpu.roll(y, LANE - 32, 1) * sin_lo
                 + pltpu.roll(y, 32, 1) * sin_hi)
        o_ref[rows, cols] = y.astype(o_ref.dtype)


def _rope_tables(n_tok):
    quarter = HEAD_DIM // 4
    t = jnp.arange(n_tok)
    inv = ROPE_BASE ** (-jnp.arange(quarter, dtype=F32) / quarter)

    def ang(p):
        a = p.astype(F32)[:, None] * inv[None, :]
        return jnp.concatenate([a, a], axis=-1)

    angles = jnp.concatenate([ang(t // GRID_W), ang(t % GRID_W)], axis=-1)
    cos, sin = jnp.cos(angles), jnp.sin(angles)
    low = (jnp.arange(HEAD_DIM) % (2 * quarter)) < quarter
    return cos, jnp.where(low, -sin, 0.0), jnp.where(low, 0.0, sin)


def _qkv_proj(h, w, layer, gains, group, *, rope, out_dtype):
    extras = [gains]
    specs = [pl.BlockSpec((1, MM_TN), lambda i, j: (0, j))]
    if rope:
        tiles_per_seq = group.seq_len // MM_TM
        tab = pl.BlockSpec((MM_TM, HEAD_DIM), lambda i, j: (i % tiles_per_seq, 0))
        extras += list(_rope_tables(group.seq_len))
        specs += [tab, tab, tab]
    epi = functools.partial(_qkv_epilogue, rope=rope)
    return _matmul(h, w, layer, epi, tuple(extras), specs, out_dtype, name="qkv_proj")


def _lam_value(lam_ref, lam_init):
    v = lam_ref[...]
    t1 = jnp.sum(v[0:1] * v[1:2], axis=-1, keepdims=True)
    t2 = jnp.sum(v[2:3] * v[3:4], axis=-1, keepdims=True)
    return jnp.exp(t1) - jnp.exp(t2) + lam_init


def _scores(q, k):
    return lax.dot_general(q.astype(BF16), k.astype(BF16), _NT,
                           preferred_element_type=F32)


def _softmax(s):
    e = jnp.exp2(s - jnp.max(s, axis=-1, keepdims=True))
    return e / jnp.sum(e, axis=-1, keepdims=True)


def _joint_exp(s_a, s_b):
    m = jnp.maximum(jnp.max(s_a, axis=-1, keepdims=True),
                    jnp.max(s_b, axis=-1, keepdims=True))
    e_a = jnp.exp2(s_a - m)
    e_b = jnp.exp2(s_b - m)
    denom = (jnp.sum(e_a, axis=-1, keepdims=True)
             + jnp.sum(e_b, axis=-1, keepdims=True))
    return e_a.astype(BF16), e_b.astype(BF16), denom


def _subln(o, g_ref, lam_init):
    return _rms(o) * g_ref[...] * (1.0 - lam_init)


def _prompt_attn_body(lam_ref, q0, q1, k0, k1, va, qb, kb, vb, g_ref,
                      oa_ref, ob_ref, *, lam_init):
    lam = _lam_value(lam_ref, lam_init)
    a = _softmax(_scores(q0[...], k0[...])) - lam * _softmax(_scores(q1[...], k1[...]))
    o = jnp.dot(a.astype(BF16), va[...].astype(BF16), preferred_element_type=F32)
    oa_ref[...] = _subln(o, g_ref, lam_init).astype(oa_ref.dtype)
    for t in range(2):
        cols = slice(t * HEAD_DIM, (t + 1) * HEAD_DIM)
        p = _softmax(_scores(qb[:, cols], kb[:, cols]))
        ob_ref[:, cols] = jnp.dot(p.astype(BF16), vb[:, cols].astype(BF16),
                                  preferred_element_type=F32).astype(ob_ref.dtype)


def _prompt_attn(qkv, lam_vecs, subln_g, lam_init, seq_len):
    rows = qkv.shape[0]
    nb = rows // seq_len
    sec128 = SECTION // HEAD_DIM
    sec256 = SECTION // (2 * HEAD_DIM)

    def blk(width, col0):
        return pl.BlockSpec((seq_len, width), lambda b, h: (b, col0 + h))

    half = sec128 // 2
    in_specs = [
        pl.BlockSpec((4, HEAD_DIM), lambda b, h: (0, 0)),
        blk(HEAD_DIM, 0), blk(HEAD_DIM, half),
        blk(HEAD_DIM, sec128), blk(HEAD_DIM, sec128 + half),
        blk(2 * HEAD_DIM, 2 * sec256),
        blk(2 * HEAD_DIM, 3 * sec256),
        blk(2 * HEAD_DIM, 4 * sec256),
        blk(2 * HEAD_DIM, 5 * sec256),
        pl.BlockSpec((1, 2 * HEAD_DIM), lambda b, h: (0, 0)),
    ]
    out_spec = pl.BlockSpec((seq_len, 2 * HEAD_DIM), lambda b, h: (b, h))
    out = jax.ShapeDtypeStruct((rows, SECTION), BF16)
    return pl.pallas_call(
        functools.partial(_prompt_attn_body, lam_init=lam_init),
        grid=(nb, N_HEADS_A),
        in_specs=in_specs,
        out_specs=[out_spec, out_spec],
        out_shape=[out, out],
        compiler_params=_params(("parallel", "parallel")),
        name="prompt_attn",
    )(lam_vecs, qkv, qkv, qkv, qkv, qkv, qkv, qkv, qkv, subln_g)


DIFF_KEY_CHUNK = 1024


def _latent_diff_body(lam_ref, q0, q1, k0, k1, v, kc0, kc1, vc, g_ref, o_ref,
                      *, lam_init):
    lam = _lam_value(lam_ref, lam_init)
    v_ctx = vc[...].astype(BF16)
    n_lat = k0.shape[0]

    def attend(q, k, kc):
        qv = q[...]
        s = _scores(qv, kc[...])
        m = jnp.max(s, axis=-1, keepdims=True)
        e = jnp.exp2(s - m)
        denom = jnp.sum(e, axis=-1, keepdims=True)
        acc = jnp.dot(e.astype(BF16), v_ctx, preferred_element_type=F32)
        for c in range(n_lat // DIFF_KEY_CHUNK):
            keys = slice(c * DIFF_KEY_CHUNK, (c + 1) * DIFF_KEY_CHUNK)
            s = _scores(qv, k[keys, :])
            m_new = jnp.maximum(m, jnp.max(s, axis=-1, keepdims=True))
            alpha = jnp.exp2(m - m_new)
            e = jnp.exp2(s - m_new)
            denom = denom * alpha + jnp.sum(e, axis=-1, keepdims=True)
            acc = acc * alpha + jnp.dot(e.astype(BF16), v[keys, :], preferred_element_type=F32)
            m = m_new
        return acc / denom

    o = attend(q0, k0, kc0) - lam * attend(q1, k1, kc1)
    o_ref[...] = _subln(o, g_ref, lam_init).astype(o_ref.dtype)


def _latent_diff_attn(qkv, cache_k, cache_v, lam_vecs, subln_g, lam_init,
                      seq_len, ctx_len):
    rows = qkv.shape[0]
    nb = rows // seq_len
    tq = 512
    nq = seq_len // tq
    sec128 = SECTION // HEAD_DIM
    sec256 = SECTION // (2 * HEAD_DIM)
    half = sec128 // 2

    def qblk(col0):
        return pl.BlockSpec((tq, HEAD_DIM), lambda b, h, i: (b * nq + i, col0 + h))

    def kvblk(n, width, col0):
        return pl.BlockSpec((n, width), lambda b, h, i: (b, col0 + h))

    in_specs = [
        pl.BlockSpec((4, HEAD_DIM), lambda b, h, i: (0, 0)),
        qblk(0), qblk(half),
        kvblk(seq_len, HEAD_DIM, sec128), kvblk(seq_len, HEAD_DIM, sec128 + half),
        kvblk(seq_len, 2 * HEAD_DIM, 2 * sec256),
        kvblk(ctx_len, HEAD_DIM, 0), kvblk(ctx_len, HEAD_DIM, half),
        kvblk(ctx_len, 2 * HEAD_DIM, 0),
        pl.BlockSpec((1, 2 * HEAD_DIM), lambda b, h, i: (0, 0)),
    ]
    return pl.pallas_call(
        functools.partial(_latent_diff_body, lam_init=lam_init),
        grid=(nb, N_HEADS_A, nq),
        in_specs=in_specs,
        out_specs=pl.BlockSpec((tq, 2 * HEAD_DIM), lambda b, h, i: (b * nq + i, h)),
        out_shape=jax.ShapeDtypeStruct((rows, SECTION), BF16),
        compiler_params=_params(("parallel", "parallel", "arbitrary")),
        name="latent_diff_attn",
    )(lam_vecs, qkv, qkv, qkv, qkv, qkv, cache_k, cache_k, cache_v, subln_g)


NA_Q_ROWS = 8
NA_K_ROWS = 16


def _na_band_start(i, grid_rows):
    return jnp.clip(i * NA_Q_ROWS - NA_ROWS // 2, 0, grid_rows - NA_K_ROWS)


NA_D_MIN = -NA_Q_ROWS
NA_D_MAX = NA_K_ROWS + NA_ROWS - 2


def _na_pair_tables(rpb):
    n_h, n_dr, n_dc = rpb.shape
    u = jnp.concatenate([rpb[..., NA_COLS - 1:],
                         jnp.zeros((n_h, n_dr, LANE - n_dc), F32),
                         rpb[..., :NA_COLS - 1]], axis=-1)
    skew = jnp.tile(u, (1, 1, GRID_W))[..., :GRID_W * (LANE - 1)]
    toep = skew.reshape(n_h, n_dr, GRID_W, LANE - 1)[..., :GRID_W]
    cols = jnp.arange(GRID_W)
    cstart = jnp.clip(cols - NA_COLS // 2, 0, GRID_W - NA_COLS)
    col_ok = (cols[None, :] >= cstart[:, None]) & (cols[None, :] < cstart[:, None] + NA_COLS)
    toep = jnp.where(col_ok, toep, MASK_VALUE)
    ext = jnp.pad(toep, ((0, 0), (-NA_D_MIN, NA_D_MAX + 1 - n_dr), (0, 0), (0, 0)),
                  constant_values=MASK_VALUE)
    return jnp.concatenate([ext[:, :-1], ext[:, 1:]], axis=-1)


def _latent_na_body(q, k, v, kc, vc, tab_ref, o_ref, bias_ref, *, grid_rows):
    i = pl.program_id(1)
    n_blocks = grid_rows // NA_Q_ROWS
    row0 = i * NA_Q_ROWS
    band0 = _na_band_start(i, grid_rows)

    @pl.when(jnp.logical_or(i <= 1, i == n_blocks - 1))
    def _():
        lane = lax.broadcasted_iota(jnp.int32, (GRID_W, LANE), 1)
        for rq in range(NA_Q_ROWS):
            r = row0 + rq
            rs = jnp.clip(r - NA_ROWS // 2, 0, grid_rows - NA_ROWS)

            def penalty(kr):
                ok = jnp.logical_and(kr >= rs, kr < rs + NA_ROWS)
                return jnp.where(ok, 0.0, MASK_VALUE)

            for p in range(NA_K_ROWS // 2):
                kr = band0 + 2 * p
                d = kr - r + NA_ROWS - 1
                pen = jnp.where(lane < GRID_W, penalty(kr), penalty(kr + 1))
                bias_ref[rq * GRID_W:(rq + 1) * GRID_W, p * LANE:(p + 1) * LANE] = (
                    tab_ref[d - NA_D_MIN] * LOG2E + pen)

    start = pl.multiple_of(band0 * GRID_W, GRID_W)
    band = pl.ds(start, NA_K_ROWS * GRID_W)
    half = q.shape[1] // 2
    for b in range(q.shape[0]):
        k_band, v_band = k[b, band, :], v[b, band, :]
        k_ctx, v_ctx = kc[b], vc[b].astype(BF16)
        for part in range(2):
            rows = slice(part * half, (part + 1) * half)
            e_loc, e_ctx, denom = _joint_exp(_scores(q[b, rows, :], k_band) + bias_ref[rows, :],
                                             _scores(q[b, rows, :], k_ctx))
            o = (jnp.dot(e_loc, v_band, preferred_element_type=F32)
                 + jnp.dot(e_ctx, v_ctx, preferred_element_type=F32))
            o_ref[b, rows, :] = (o / denom).astype(o_ref.dtype)


def _latent_na_attn(qkv, cache_k, cache_v, rpb, seq_len, ctx_len):
    rows = qkv.shape[0]
    nb = rows // seq_len
    grid_rows = seq_len // GRID_W
    n_blocks = grid_rows // NA_Q_ROWS
    tq = NA_Q_ROWS * GRID_W
    sec128 = SECTION // HEAD_DIM
    tables = _na_pair_tables(rpb)

    qkv3 = qkv.reshape(nb, seq_len, qkv.shape[1])
    ck3 = cache_k.reshape(nb, ctx_len, SECTION)
    cv3 = cache_v.reshape(nb, ctx_len, SECTION)

    def kvblk(n, col0):
        return pl.BlockSpec((nb, n, HEAD_DIM), lambda h, i: (0, 0, col0 + h))

    in_specs = [
        pl.BlockSpec((nb, tq, HEAD_DIM), lambda h, i: (0, i, 3 * sec128 + h)),
        kvblk(seq_len, 4 * sec128), kvblk(seq_len, 5 * sec128),
        kvblk(ctx_len, 0), kvblk(ctx_len, 0),
        pl.BlockSpec((None,) + tables.shape[1:], lambda h, i: (h, 0, 0, 0)),
    ]
    out = pl.pallas_call(
        functools.partial(_latent_na_body, grid_rows=grid_rows),
        grid=(N_HEADS_B, n_blocks),
        in_specs=in_specs,
        out_specs=pl.BlockSpec((nb, tq, HEAD_DIM), lambda h, i: (0, i, h)),
        out_shape=jax.ShapeDtypeStruct((nb, seq_len, SECTION), BF16),
        scratch_shapes=[pltpu.VMEM((tq, NA_K_ROWS * GRID_W), F32)],
        compiler_params=_params(("arbitrary", "arbitrary")),
        name="latent_na_attn",
    )(qkv3, qkv3, qkv3, ck3, cv3, tables)
    return out.reshape(rows, SECTION)


GATE_CHUNKS = 2


def _spatial_gate_body(u_ref, v_ref, g_ref, ws_ref, bs_ref, o_ref):
    width = v_ref.shape[1] // N_GROUPS
    for c in range(GATE_CHUNKS):
        rows = slice(c * CHUNK, (c + 1) * CHUNK)
        vn = (_rms(v_ref[rows, :]) * g_ref[...]).astype(BF16)
        for g in range(N_GROUPS):
            cols = slice(g * width, (g + 1) * width)
            sv = jnp.dot(ws_ref[g].astype(BF16), vn[:, cols],
                         preferred_element_type=F32) + bs_ref[:, g:g + 1]
            o_ref[rows, cols] = (u_ref[rows, cols] * sv).astype(o_ref.dtype)


def _spatial_gate(z, vnorm_g, w_sp, b_sp, layer):
    rows, two_d = z.shape
    d = two_d // 2
    tm = GATE_CHUNKS * CHUNK
    return pl.pallas_call(
        _spatial_gate_body,
        grid=(rows // tm,),
        in_specs=[pl.BlockSpec((tm, d), lambda n: (n, 0)),
                  pl.BlockSpec((tm, d), lambda n: (n, 1)),
                  pl.BlockSpec((None, 1, d), lambda n: (layer, 0, 0)),
                  pl.BlockSpec((None, N_GROUPS, CHUNK, CHUNK), lambda n: (layer, 0, 0, 0)),
                  pl.BlockSpec((None, CHUNK, N_GROUPS), lambda n: (layer, 0, 0))],
        out_specs=pl.BlockSpec((tm, d), lambda n: (n, 0)),
        out_shape=jax.ShapeDtypeStruct((rows, d), BF16),
        compiler_params=_params(("parallel",)),
        name="spatial_gate",
    )(z, z, vnorm_g.reshape(vnorm_g.shape[0], 1, d), w_sp, jnp.swapaxes(b_sp, 1, 2))


FFN_TN = 256
HALO = 32
SUB = 8
FFN_TM = 2048
FFN_CHUNK = 256


def _ffn_chunks(tm):
    return (FFN_CHUNK,) * (tm // FFN_CHUNK - 1) + (FFN_CHUNK // 2,) * 2


def _ffn_front_body(x_ref, xp_ref, xn_ref, wg_ref, wu_ref, cwg_ref, cwu_ref,
                    cbg_ref, cbu_ref, o_ref, acc_ref, *, seq_len):
    tm = x_ref.shape[0]
    i = pl.program_id(0)
    sizes = _ffn_chunks(tm)
    starts = [sum(sizes[:c]) for c in range(len(sizes) + 1)]
    n_chunks = len(sizes)
    tn = wg_ref.shape[1]
    w = jnp.concatenate([wg_ref[...].astype(BF16), wu_ref[...].astype(BF16)], axis=1)

    def project(c):
        x = x_ref[starts[c]:starts[c + 1], :]
        lo, hi = HALO + starts[c], HALO + starts[c + 1]
        if c == 0:
            x, lo = jnp.concatenate([xp_ref[...], x], axis=0), 0
        if c == n_chunks - 1:
            x, hi = jnp.concatenate([x, xn_ref[...]], axis=0), tm + 2 * HALO
        acc_ref[lo:hi, :] = jnp.dot(x, w, preferred_element_type=F32)

    def conv(col0, cw_ref, cb_ref, c, at_start, at_end):
        n = sizes[c]
        r0 = HALO + starts[c]
        cols = slice(col0, col0 + tn)
        x_prev = acc_ref[r0 - 1:r0 - 1 + n, cols]
        x_next = acc_ref[r0 + 1:r0 + 1 + n, cols]
        row = lax.broadcasted_iota(jnp.int32, (SUB, 1), 0)
        head = jnp.where(jnp.logical_and(at_start, row == 0), 0.0, x_prev[:SUB])
        tail = jnp.where(jnp.logical_and(at_end, row == SUB - 1), 0.0, x_next[n - SUB:])
        x_prev = jnp.concatenate([head, x_prev[SUB:]], axis=0)
        x_next = jnp.concatenate([x_next[:n - SUB], tail], axis=0)
        return (cw_ref[0:1, :] * x_prev + cw_ref[1:2, :] * acc_ref[r0:r0 + n, cols]
                + cw_ref[2:3, :] * x_next + cb_ref[...])

    def gate(c):
        first = i * tm + starts[c]
        at_start = first % seq_len == 0
        at_end = (first + sizes[c]) % seq_len == 0
        g = conv(0, cwg_ref, cbg_ref, c, at_start, at_end)
        u = conv(tn, cwu_ref, cbu_ref, c, at_start, at_end)
        o_ref[starts[c]:starts[c + 1], :] = (g * jax.nn.sigmoid(g) * u).astype(o_ref.dtype)

    project(0)
    for c in range(n_chunks):
        if c + 1 < n_chunks:
            project(c + 1)
        gate(c)


def _ffn_front(h, w_up, conv_w, conv_b, layer, group):
    rows, d = h.shape
    two_f = w_up.shape[-1]
    d_ff = two_f // 2
    tm, tn = min(FFN_TM, rows), FFN_TN
    nj = d_ff // tn
    assert rows % tm == 0 and tm % FFN_CHUNK == 0 and d_ff % tn == 0
    assert group.seq_len % FFN_CHUNK == 0 and (tm % group.seq_len == 0 or group.seq_len % tm == 0)
    halo_blocks = rows // HALO

    def w(off):
        return pl.BlockSpec((None, d, tn), lambda i, j: (layer, 0, off + j))

    def cw(off):
        return pl.BlockSpec((None, 3, tn), lambda i, j: (layer, 0, off + j))

    def cb(off):
        return pl.BlockSpec((None, 1, tn), lambda i, j: (layer, 0, off + j))

    in_specs = [
        pl.BlockSpec((tm, d), lambda i, j: (i, 0), pipeline_mode=pl.Buffered(1)),
        pl.BlockSpec((HALO, d), lambda i, j: (jnp.maximum(i * (tm // HALO) - 1, 0), 0)),
        pl.BlockSpec((HALO, d), lambda i, j: (jnp.minimum((i + 1) * (tm // HALO), halo_blocks - 1), 0)),
        w(0), w(nj), cw(0), cw(nj), cb(0), cb(nj),
    ]
    cb3 = conv_b.reshape(conv_b.shape[0], 1, two_f)
    return pl.pallas_call(
        functools.partial(_ffn_front_body, seq_len=group.seq_len),
        grid=(rows // tm, nj),
        in_specs=in_specs,
        out_specs=pl.BlockSpec((tm, tn), lambda i, j: (i, j)),
        out_shape=jax.ShapeDtypeStruct((rows, d_ff), BF16),
        scratch_shapes=[pltpu.VMEM((tm + 2 * HALO, 2 * tn), F32)],
        compiler_params=_params(("parallel", "arbitrary")),
        name="ffn_front",
    )(h, h, h, w_up, w_up, conv_w, conv_w, cb3, cb3)


def kernel(x_prompt, x_sample, cache_a_k, cache_a_v, cache_b_k, cache_b_v, c, c_ctx,
           w_ada, b_ada, norm1_g, norm2_g,
           w_in_even, qn_a, kn_a, lam_q1, lam_k1, lam_q2, lam_k2, subln_g, qn_b, kn_b, rpb, w_out_even,
           w_in_odd, vnorm_g, w_sp, b_sp, w_out_odd,
           w_up, conv_w, conv_b, w_down):
    batch, seq, d = x_prompt.shape
    dec_batch, dec_seq, _ = x_sample.shape
    past_len = cache_a_k.shape[2]
    depth = w_ada.shape[0]

    groups = (_Group(batch * seq, seq, 0), _Group(dec_batch * dec_seq, dec_seq, 1))
    xs = [x_prompt.reshape(batch * seq, d), x_sample.reshape(dec_batch * dec_seq, d)]

    cond8 = jnp.concatenate([c_ctx[None], c, jnp.zeros((8 - 1 - dec_batch, d), F32)])

    w_down_b = w_down.astype(BF16)

    prompt_qkv = []
    for layer in range(depth):
        mod = _adaln(cond8, w_ada, b_ada, layer).reshape(8, 6, 1, d)
        if layer % 2 == 0:
            e = layer // 2
            lam_init = 0.8 - 0.6 * math.exp(-0.3 * layer)
            lam_vecs = jnp.stack([lam_q1[e], lam_k1[e], lam_q2[e], lam_k2[e]])
            ones = jnp.ones((SECTION,), F32)
            reps = SECTION // HEAD_DIM
            gains = jnp.concatenate([jnp.tile(qn_a[e] * Q_PRESCALE, reps), jnp.tile(kn_a[e], reps), ones,
                                     jnp.tile(qn_b[e] * Q_PRESCALE, reps), jnp.tile(kn_b[e], reps), ones])[None]
            sub_g = subln_g[e][None]
            mixed = []
            for gi, group in enumerate(groups):
                h = _norm_mod(xs[gi], norm1_g, layer, mod, 0, group)
                if gi == 0:
                    qkv = _qkv_proj(h, w_in_even, e, gains, group, rope=False, out_dtype=F32)
                    prompt_qkv.append(qkv)
                    oa, ob = _prompt_attn(qkv, lam_vecs, sub_g, lam_init, group.seq_len)
                else:
                    qkv = _qkv_proj(h, w_in_even, e, gains, group, rope=True, out_dtype=BF16)
                    ck_a = cache_a_k[:, e].reshape(dec_batch * past_len, SECTION)
                    cv_a = cache_a_v[:, e].reshape(dec_batch * past_len, SECTION)
                    ck_b = cache_b_k[:, e].reshape(dec_batch * past_len, SECTION)
                    cv_b = cache_b_v[:, e].reshape(dec_batch * past_len, SECTION)
                    oa = _latent_diff_attn(qkv, ck_a, cv_a, lam_vecs, sub_g, lam_init,
                                           group.seq_len, past_len)
                    ob = _latent_na_attn(qkv, ck_b, cv_b, rpb[e], group.seq_len, past_len)
                mixed.append((oa, ob))
            for gi, group in enumerate(groups):
                xs[gi] = _matmul_resid(mixed[gi], w_out_even, e, xs[gi], mod, 2, group,
                                       name="even_out_proj")
        else:
            o = layer // 2
            for gi, group in enumerate(groups):
                h = _norm_mod(xs[gi], norm1_g, layer, mod, 0, group)
                z = _matmul(h, w_in_odd, o, _gelu_epilogue, (), [], F32, name="odd_in_proj")
                gated = _spatial_gate(z, vnorm_g, w_sp, b_sp, o)
                xs[gi] = _matmul_resid(gated, w_out_odd, o, xs[gi], mod, 2, group,
                                       name="odd_out_proj")
        for gi, group in enumerate(groups):
            h = _norm_mod(xs[gi], norm2_g, layer, mod, 3, group)
            act = _ffn_front(h, w_up, conv_w, conv_b, layer, group)
            half = _matmul(act, w_down_b, layer, _plain_epilogue, (), [], F32,
                           k_parts=2, k_part=0, name="ffn_down_lo")
            xs[gi] = _matmul_resid(act, w_down_b, layer, xs[gi], mod, 5, group, partial_sum=half,
                                   k_parts=2, k_part=1, name="ffn_down_hi")

    def state(col0, shape):
        parts = [q[:, col0:col0 + SECTION].reshape((batch, seq) + shape) for q in prompt_qkv]
        return jnp.stack(parts, axis=1)

    h_a, h_b = N_HEADS_A, N_HEADS_B
    return (xs[0].reshape(batch, seq, d),
            xs[1].reshape(dec_batch, dec_seq, d),
            state(1 * SECTION, (2, h_a, HEAD_DIM)),
            state(2 * SECTION, (h_a, 2 * HEAD_DIM)),
            state(4 * SECTION, (h_b, HEAD_DIM)),
            state(5 * SECTION, (h_b, HEAD_DIM)))
```

```python
import functools
import math

import jax
import jax.numpy as jnp
from jax import lax
from jax.experimental import pallas as pl
from jax.experimental.pallas import tpu as pltpu

F32 = jnp.float32
BF16 = jnp.bfloat16

NORM_EPS = 1e-6
ROPE_BASE = 10000.0
GRID_W = 64
HEAD_DIM = 128
N_HEADS_A = 8
N_HEADS_B = 16
SECTION = 2048
NA_ROWS = 8
NA_COLS = 16
CHUNK = 128
N_GROUPS = 8
MASK_VALUE = -1e30
LOG2E = math.log2(math.e)
Q_PRESCALE = HEAD_DIM ** -0.5 * LOG2E
LANE = 128
VMEM_LIMIT = 56 * 1024 * 1024

_NT = (((1,), (1,)), ((), ()))


def _params(semantics):
    return pltpu.CompilerParams(dimension_semantics=semantics,
                                vmem_limit_bytes=VMEM_LIMIT)


def _rms(x):
    return x * lax.rsqrt(jnp.mean(x * x, axis=-1, keepdims=True) + NORM_EPS)


def _adaln_body(c_ref, w_ref, b_ref, o_ref):
    c = c_ref[...]
    s = c * jax.nn.sigmoid(c)
    o_ref[...] = jnp.dot(s.astype(BF16), w_ref[...].astype(BF16),
                         preferred_element_type=F32) + b_ref[...]


def _adaln(cond8, w_ada, b_ada, layer):
    _, d, n = w_ada.shape
    tn = 1024
    return pl.pallas_call(
        _adaln_body,
        grid=(n // tn,),
        in_specs=[pl.BlockSpec((8, d), lambda j: (0, 0)),
                  pl.BlockSpec((None, d, tn), lambda j: (layer, 0, j)),
                  pl.BlockSpec((None, 1, tn), lambda j: (layer, 0, j))],
        out_specs=pl.BlockSpec((8, tn), lambda j: (0, j)),
        out_shape=jax.ShapeDtypeStruct((8, n), F32),
        compiler_params=_params(("parallel",)),
        name="adaln",
    )(cond8, w_ada, b_ada.reshape(b_ada.shape[0], 1, n))


class _Group:
    def __init__(self, rows, seq_len, cond_base):
        self.rows = rows
        self.seq_len = seq_len
        self.cond_base = cond_base

    def cond(self, row0):
        if self.cond_base == 0:
            return 0
        return self.cond_base + row0 // self.seq_len


def _mod_spec(group, tm, tn, which, col_of):
    return pl.BlockSpec(
        (None, None, 1, tn),
        lambda *ids: (group.cond(ids[0] * tm), which, 0, col_of(*ids)))


NORM_TM = 512
NORM_SUB = 16


def _norm_mod_body(x_ref, g_ref, sh_ref, sc_ref, o_ref):
    gain, scale, shift = g_ref[...], 1 + sc_ref[...], sh_ref[...]
    for r in range(0, x_ref.shape[0], NORM_SUB):
        y = _rms(x_ref[r:r + NORM_SUB, :]) * gain
        o_ref[r:r + NORM_SUB, :] = (y * scale + shift).astype(o_ref.dtype)


def _norm_mod(x, gains, layer, mod, shift_idx, group):
    rows, d = x.shape
    tm = NORM_TM
    return pl.pallas_call(
        _norm_mod_body,
        grid=(rows // tm,),
        in_specs=[pl.BlockSpec((tm, d), lambda i: (i, 0)),
                  pl.BlockSpec((None, 1, d), lambda i: (layer, 0, 0)),
                  _mod_spec(group, tm, d, shift_idx, lambda i: 0),
                  _mod_spec(group, tm, d, shift_idx + 1, lambda i: 0)],
        out_specs=pl.BlockSpec((tm, d), lambda i: (i, 0)),
        out_shape=jax.ShapeDtypeStruct((rows, d), BF16),
        compiler_params=_params(("parallel",)),
        name="norm_mod",
    )(x, gains.reshape(gains.shape[0], 1, d), mod, mod)


MM_TM = 1024
MM_TN = 512
MM_CHUNK = 256


def _mm_body(*refs, n_x, n_extra, epilogue):
    x_refs, w_ref = refs[:n_x], refs[n_x]
    extras = refs[n_x + 1:n_x + 1 + n_extra]
    o_ref = refs[n_x + 1 + n_extra]
    j = pl.program_id(1)
    w = w_ref[...].astype(BF16)
    tm = x_refs[0].shape[0]
    sizes = (MM_CHUNK,) * (tm // MM_CHUNK - 1) + (MM_CHUNK // 2,) * 2
    for c in range(len(sizes)):
        rows = slice(sum(sizes[:c]), sum(sizes[:c + 1]))
        acc, k0 = None, 0
        for x_ref in x_refs:
            k1 = k0 + x_ref.shape[1]
            part = jnp.dot(x_ref[rows, :], w[k0:k1, :], preferred_element_type=F32)
            acc = part if acc is None else acc + part
            k0 = k1
        epilogue(acc, extras, o_ref, j, rows)


def _matmul(x, w, layer, epilogue, extras, extra_specs, out_dtype, *,
            k_parts=1, k_part=0, name):
    xs = x if isinstance(x, tuple) else (x,)
    assert len(xs) == 1 or k_parts == 1
    rows = xs[0].shape[0]
    kdim = sum(p.shape[1] for p in xs)
    n = w.shape[-1]
    tm, tn = MM_TM, MM_TN
    tk = kdim // k_parts
    body = functools.partial(_mm_body, n_x=len(xs), n_extra=len(extras), epilogue=epilogue)
    x_specs = [pl.BlockSpec((tm, p.shape[1] // k_parts), lambda i, j: (i, k_part)) for p in xs]
    return pl.pallas_call(
        body,
        grid=(rows // tm, n // tn),
        in_specs=[*x_specs,
                  pl.BlockSpec((None, tk, tn), lambda i, j: (layer, k_part, j)),
                  *extra_specs],
        out_specs=pl.BlockSpec((tm, tn), lambda i, j: (i, j)),
        out_shape=jax.ShapeDtypeStruct((rows, n), out_dtype),
        compiler_params=_params(("parallel", "arbitrary")),
        name=name,
    )(*xs, w, *extras)


def _plain_epilogue(acc, extras, o_ref, j, rows):
    o_ref[rows, :] = acc.astype(o_ref.dtype)


def _gelu_epilogue(acc, extras, o_ref, j, rows):
    o_ref[rows, :] = (0.5 * acc * (1 + lax.erf(acc * math.sqrt(0.5)))).astype(o_ref.dtype)


def _resid_epilogue(acc, extras, o_ref, j, rows):
    x_ref, gate_ref = extras[:2]
    if len(extras) == 3:
        acc = extras[2][rows, :] + acc
    o_ref[rows, :] = x_ref[rows, :] + gate_ref[...] * acc


def _matmul_resid(h, w, layer, x, mod, gate_idx, group, *, partial_sum=None,
                  k_parts=1, k_part=0, name):
    tile = pl.BlockSpec((MM_TM, MM_TN), lambda i, j: (i, j))
    specs = [tile, _mod_spec(group, MM_TM, MM_TN, gate_idx, lambda i, j: j)]
    extras = (x, mod)
    if partial_sum is not None:
        specs.append(tile)
        extras += (partial_sum,)
    return _matmul(h, w, layer, _resid_epilogue, extras, specs, F32,
                   k_parts=k_parts, k_part=k_part, name=name)


def _qkv_epilogue(acc, extras, o_ref, j, rows, *, rope):
    g_ref = extras[0]
    sec = (j * MM_TN) // SECTION
    is_v = jnp.logical_or(sec == 2, sec == 5)
    if rope:
        is_rope = sec < 2
        cos = jnp.where(is_rope, extras[1][rows, :], 1.0)
        sin_lo = jnp.where(is_rope, extras[2][rows, :], 0.0)
        sin_hi = jnp.where(is_rope, extras[3][rows, :], 0.0)
    for c in range(MM_TN // LANE):
        cols = slice(c * LANE, (c + 1) * LANE)
        a = acc[:, cols]
        y = jnp.where(is_v, a, _rms(a) * g_ref[:, cols])
        if rope:
            y = (y * cos + pltpu.roll(y, LANE - 32, 1) * sin_lo
                 + pltpu.roll(y, 32, 1) * sin_hi)
        o_ref[rows, cols] = y.astype(o_ref.dtype)


def _rope_tables(n_tok):
    quarter = HEAD_DIM // 4
    t = jnp.arange(n_tok)
    inv = ROPE_BASE ** (-jnp.arange(quarter, dtype=F32) / quarter)

    def ang(p):
        a = p.astype(F32)[:, None] * inv[None, :]
        return jnp.concatenate([a, a], axis=-1)

    angles = jnp.concatenate([ang(t // GRID_W), ang(t % GRID_W)], axis=-1)
    cos, sin = jnp.cos(angles), jnp.sin(angles)
    low = (jnp.arange(HEAD_DIM) % (2 * quarter)) < quarter
    return cos, jnp.where(low, -sin, 0.0), jnp.where(low, 0.0, sin)


def _qkv_proj(h, w, layer, gains, group, *, rope, out_dtype):
    extras = [gains]
    specs = [pl.BlockSpec((1, MM_TN), lambda i, j: (0, j))]
    if rope:
        tiles_per_seq = group.seq_len // MM_TM
        tab = pl.BlockSpec((MM_TM, HEAD_DIM), lambda i, j: (i % tiles_per_seq, 0))
        extras += list(_rope_tables(group.seq_len))
        specs += [tab, tab, tab]
    epi = functools.partial(_qkv_epilogue, rope=rope)
    return _matmul(h, w, layer, epi, tuple(extras), specs, out_dtype, name="qkv_proj")


def _lam_value(lam_ref, lam_init):
    v = lam_ref[...]
    t1 = jnp.sum(v[0:1] * v[1:2], axis=-1, keepdims=True)
    t2 = jnp.sum(v[2:3] * v[3:4], axis=-1, keepdims=True)
    return jnp.exp(t1) - jnp.exp(t2) + lam_init


def _scores(q, k):
    return lax.dot_general(q.astype(BF16), k.astype(BF16), _NT,
                           preferred_element_type=F32)


def _softmax(s):
    e = jnp.exp2(s - jnp.max(s, axis=-1, keepdims=True))
    return e / jnp.sum(e, axis=-1, keepdims=True)


def _joint_exp(s_a, s_b):
    m = jnp.maximum(jnp.max(s_a, axis=-1, keepdims=True),
                    jnp.max(s_b, axis=-1, keepdims=True))
    e_a = jnp.exp2(s_a - m)
    e_b = jnp.exp2(s_b - m)
    denom = (jnp.sum(e_a, axis=-1, keepdims=True)
             + jnp.sum(e_b, axis=-1, keepdims=True))
    return e_a.astype(BF16), e_b.astype(BF16), denom


def _subln(o, g_ref, lam_init):
    return _rms(o) * g_ref[...] * (1.0 - lam_init)


def _prompt_attn_body(lam_ref, q0, q1, k0, k1, va, qb, kb, vb, g_ref,
                      oa_ref, ob_ref, *, lam_init):
    lam = _lam_value(lam_ref, lam_init)
    a = _softmax(_scores(q0[...], k0[...])) - lam * _softmax(_scores(q1[...], k1[...]))
    o = jnp.dot(a.astype(BF16), va[...].astype(BF16), preferred_element_type=F32)
    oa_ref[...] = _subln(o, g_ref, lam_init).astype(oa_ref.dtype)
    for t in range(2):
        cols = slice(t * HEAD_DIM, (t + 1) * HEAD_DIM)
        p = _softmax(_scores(qb[:, cols], kb[:, cols]))
        ob_ref[:, cols] = jnp.dot(p.astype(BF16), vb[:, cols].astype(BF16),
                                  preferred_element_type=F32).astype(ob_ref.dtype)


def _prompt_attn(qkv, lam_vecs, subln_g, lam_init, seq_len):
    rows = qkv.shape[0]
    nb = rows // seq_len
    sec128 = SECTION // HEAD_DIM
    sec256 = SECTION // (2 * HEAD_DIM)

    def blk(width, col0):
        return pl.BlockSpec((seq_len, width), lambda b, h: (b, col0 + h))

    half = sec128 // 2
    in_specs = [
        pl.BlockSpec((4, HEAD_DIM), lambda b, h: (0, 0)),
        blk(HEAD_DIM, 0), blk(HEAD_DIM, half),
        blk(HEAD_DIM, sec128), blk(HEAD_DIM, sec128 + half),
        blk(2 * HEAD_DIM, 2 * sec256),
        blk(2 * HEAD_DIM, 3 * sec256),
        blk(2 * HEAD_DIM, 4 * sec256),
        blk(2 * HEAD_DIM, 5 * sec256),
        pl.BlockSpec((1, 2 * HEAD_DIM), lambda b, h: (0, 0)),
    ]
    out_spec = pl.BlockSpec((seq_len, 2 * HEAD_DIM), lambda b, h: (b, h))
    out = jax.ShapeDtypeStruct((rows, SECTION), BF16)
    return pl.pallas_call(
        functools.partial(_prompt_attn_body, lam_init=lam_init),
        grid=(nb, N_HEADS_A),
        in_specs=in_specs,
        out_specs=[out_spec, out_spec],
        out_shape=[out, out],
        compiler_params=_params(("parallel", "parallel")),
        name="prompt_attn",
    )(lam_vecs, qkv, qkv, qkv, qkv, qkv, qkv, qkv, qkv, subln_g)


DIFF_KEY_CHUNK = 1024


def _latent_diff_body(lam_ref, q0, q1, k0, k1, v, kc0, kc1, vc, g_ref, o_ref,
                      *, lam_init):
    lam = _lam_value(lam_ref, lam_init)
    v_ctx = vc[...].astype(BF16)
    n_lat = k0.shape[0]

    tq = q0.shape[0]
    qa, qb = q0[...], q1[...]

    def stacked_scores(ka, kb):
        return jnp.concatenate([_scores(qa, ka), _scores(qb, kb)], axis=0)

    s = stacked_scores(kc0[...], kc1[...])
    m = jnp.max(s, axis=-1, keepdims=True)
    e = jnp.exp2(s - m)
    denom = jnp.sum(e, axis=-1, keepdims=True)
    acc = jnp.dot(e.astype(BF16), v_ctx, preferred_element_type=F32)
    for c in range(n_lat // DIFF_KEY_CHUNK):
        keys = slice(c * DIFF_KEY_CHUNK, (c + 1) * DIFF_KEY_CHUNK)
        s = stacked_scores(k0[keys, :], k1[keys, :])
        m_new = jnp.maximum(m, jnp.max(s, axis=-1, keepdims=True))
        alpha = jnp.exp2(m - m_new)
        e = jnp.exp2(s - m_new)
        denom = denom * alpha + jnp.sum(e, axis=-1, keepdims=True)
        acc = acc * alpha + jnp.dot(e.astype(BF16), v[keys, :], preferred_element_type=F32)
        m = m_new
    out = acc / denom
    o = out[:tq] - lam * out[tq:]
    o_ref[...] = _subln(o, g_ref, lam_init).astype(o_ref.dtype)


def _latent_diff_attn(qkv, cache_k, cache_v, lam_vecs, subln_g, lam_init,
                      seq_len, ctx_len):
    rows = qkv.shape[0]
    nb = rows // seq_len
    tq = 512
    nq = seq_len // tq
    sec128 = SECTION // HEAD_DIM
    sec256 = SECTION // (2 * HEAD_DIM)
    half = sec128 // 2

    def qblk(col0):
        return pl.BlockSpec((tq, HEAD_DIM), lambda b, h, i: (b * nq + i, col0 + h))

    def kvblk(n, width, col0):
        return pl.BlockSpec((n, width), lambda b, h, i: (b, col0 + h))

    in_specs = [
        pl.BlockSpec((4, HEAD_DIM), lambda b, h, i: (0, 0)),
        qblk(0), qblk(half),
        kvblk(seq_len, HEAD_DIM, sec128), kvblk(seq_len, HEAD_DIM, sec128 + half),
        kvblk(seq_len, 2 * HEAD_DIM, 2 * sec256),
        kvblk(ctx_len, HEAD_DIM, 0), kvblk(ctx_len, HEAD_DIM, half),
        kvblk(ctx_len, 2 * HEAD_DIM, 0),
        pl.BlockSpec((1, 2 * HEAD_DIM), lambda b, h, i: (0, 0)),
    ]
    return pl.pallas_call(
        functools.partial(_latent_diff_body, lam_init=lam_init),
        grid=(nb, N_HEADS_A, nq),
        in_specs=in_specs,
        out_specs=pl.BlockSpec((tq, 2 * HEAD_DIM), lambda b, h, i: (b * nq + i, h)),
        out_shape=jax.ShapeDtypeStruct((rows, SECTION), BF16),
        compiler_params=_params(("parallel", "parallel", "arbitrary")),
        name="latent_diff_attn",
    )(lam_vecs, qkv, qkv, qkv, qkv, qkv, cache_k, cache_k, cache_v, subln_g)


NA_Q_ROWS = 8
NA_K_ROWS = 16


def _na_band_start(i, grid_rows):
    return jnp.clip(i * NA_Q_ROWS - NA_ROWS // 2, 0, grid_rows - NA_K_ROWS)


NA_D_MIN = -NA_Q_ROWS
NA_D_MAX = NA_K_ROWS + NA_ROWS - 2


def _na_pair_tables(rpb):
    n_h, n_dr, n_dc = rpb.shape
    u = jnp.concatenate([rpb[..., NA_COLS - 1:],
                         jnp.zeros((n_h, n_dr, LANE - n_dc), F32),
                         rpb[..., :NA_COLS - 1]], axis=-1)
    skew = jnp.tile(u, (1, 1, GRID_W))[..., :GRID_W * (LANE - 1)]
    toep = skew.reshape(n_h, n_dr, GRID_W, LANE - 1)[..., :GRID_W]
    cols = jnp.arange(GRID_W)
    cstart = jnp.clip(cols - NA_COLS // 2, 0, GRID_W - NA_COLS)
    col_ok = (cols[None, :] >= cstart[:, None]) & (cols[None, :] < cstart[:, None] + NA_COLS)
    toep = jnp.where(col_ok, toep, MASK_VALUE)
    ext = jnp.pad(toep, ((0, 0), (-NA_D_MIN, NA_D_MAX + 1 - n_dr), (0, 0), (0, 0)),
                  constant_values=MASK_VALUE)
    return jnp.concatenate([ext[:, :-1], ext[:, 1:]], axis=-1)


def _latent_na_body(q, k, v, kc, vc, tab_ref, o_ref, bias_ref, *, grid_rows):
    i = pl.program_id(1)
    n_blocks = grid_rows // NA_Q_ROWS
    row0 = i * NA_Q_ROWS
    band0 = _na_band_start(i, grid_rows)

    @pl.when(jnp.logical_or(i <= 1, i == n_blocks - 1))
    def _():
        lane = lax.broadcasted_iota(jnp.int32, (GRID_W, LANE), 1)
        for rq in range(NA_Q_ROWS):
            r = row0 + rq
            rs = jnp.clip(r - NA_ROWS // 2, 0, grid_rows - NA_ROWS)

            def penalty(kr):
                ok = jnp.logical_and(kr >= rs, kr < rs + NA_ROWS)
                return jnp.where(ok, 0.0, MASK_VALUE)

            for p in range(NA_K_ROWS // 2):
                kr = band0 + 2 * p
                d = kr - r + NA_ROWS - 1
                pen = jnp.where(lane < GRID_W, penalty(kr), penalty(kr + 1))
                bias_ref[rq * GRID_W:(rq + 1) * GRID_W, p * LANE:(p + 1) * LANE] = (
                    tab_ref[d - NA_D_MIN] * LOG2E + pen)

    start = pl.multiple_of(band0 * GRID_W, GRID_W)
    band = pl.ds(start, NA_K_ROWS * GRID_W)
    half = q.shape[1] // 2
    for b in range(q.shape[0]):
        k_band, v_band = k[b, band, :], v[b, band, :]
        k_ctx, v_ctx = kc[b], vc[b].astype(BF16)
        for part in range(2):
            rows = slice(part * half, (part + 1) * half)
            e_loc, e_ctx, denom = _joint_exp(_scores(q[b, rows, :], k_band) + bias_ref[rows, :],
                                             _scores(q[b, rows, :], k_ctx))
            o = (jnp.dot(e_loc, v_band, preferred_element_type=F32)
                 + jnp.dot(e_ctx, v_ctx, preferred_element_type=F32))
            o_ref[b, rows, :] = (o / denom).astype(o_ref.dtype)


def _latent_na_attn(qkv, cache_k, cache_v, rpb, seq_len, ctx_len):
    rows = qkv.shape[0]
    nb = rows // seq_len
    grid_rows = seq_len // GRID_W
    n_blocks = grid_rows // NA_Q_ROWS
    tq = NA_Q_ROWS * GRID_W
    sec128 = SECTION // HEAD_DIM
    tables = _na_pair_tables(rpb)

    qkv3 = qkv.reshape(nb, seq_len, qkv.shape[1])
    ck3 = cache_k.reshape(nb, ctx_len, SECTION)
    cv3 = cache_v.reshape(nb, ctx_len, SECTION)

    def kvblk(n, col0):
        return pl.BlockSpec((nb, n, HEAD_DIM), lambda h, i: (0, 0, col0 + h))

    in_specs = [
        pl.BlockSpec((nb, tq, HEAD_DIM), lambda h, i: (0, i, 3 * sec128 + h)),
        kvblk(seq_len, 4 * sec128), kvblk(seq_len, 5 * sec128),
        kvblk(ctx_len, 0), kvblk(ctx_len, 0),
        pl.BlockSpec((None,) + tables.shape[1:], lambda h, i: (h, 0, 0, 0)),
    ]
    out = pl.pallas_call(
        functools.partial(_latent_na_body, grid_rows=grid_rows),
        grid=(N_HEADS_B, n_blocks),
        in_specs=in_specs,
        out_specs=pl.BlockSpec((nb, tq, HEAD_DIM), lambda h, i: (0, i, h)),
        out_shape=jax.ShapeDtypeStruct((nb, seq_len, SECTION), BF16),
        scratch_shapes=[pltpu.VMEM((tq, NA_K_ROWS * GRID_W), F32)],
        compiler_params=_params(("arbitrary", "arbitrary")),
        name="latent_na_attn",
    )(qkv3, qkv3, qkv3, ck3, cv3, tables)
    return out.reshape(rows, SECTION)


GATE_CHUNKS = 2


def _spatial_gate_body(u_ref, v_ref, g_ref, ws_ref, bs_ref, o_ref):
    width = v_ref.shape[1] // N_GROUPS
    for c in range(GATE_CHUNKS):
        rows = slice(c * CHUNK, (c + 1) * CHUNK)
        vn = (_rms(v_ref[rows, :]) * g_ref[...]).astype(BF16)
        for g in range(N_GROUPS):
            cols = slice(g * width, (g + 1) * width)
            sv = jnp.dot(ws_ref[g].astype(BF16), vn[:, cols],
                         preferred_element_type=F32) + bs_ref[:, g:g + 1]
            o_ref[rows, cols] = (u_ref[rows, cols] * sv).astype(o_ref.dtype)


def _spatial_gate(z, vnorm_g, w_sp, b_sp, layer):
    rows, two_d = z.shape
    d = two_d // 2
    tm = GATE_CHUNKS * CHUNK
    return pl.pallas_call(
        _spatial_gate_body,
        grid=(rows // tm,),
        in_specs=[pl.BlockSpec((tm, d), lambda n: (n, 0)),
                  pl.BlockSpec((tm, d), lambda n: (n, 1)),
                  pl.BlockSpec((None, 1, d), lambda n: (layer, 0, 0)),
                  pl.BlockSpec((None, N_GROUPS, CHUNK, CHUNK), lambda n: (layer, 0, 0, 0)),
                  pl.BlockSpec((None, CHUNK, N_GROUPS), lambda n: (layer, 0, 0))],
        out_specs=pl.BlockSpec((tm, d), lambda n: (n, 0)),
        out_shape=jax.ShapeDtypeStruct((rows, d), BF16),
        compiler_params=_params(("parallel",)),
        name="spatial_gate",
    )(z, z, vnorm_g.reshape(vnorm_g.shape[0], 1, d), w_sp, jnp.swapaxes(b_sp, 1, 2))


FFN_TN = 256
HALO = 32
SUB = 8
FFN_TM = 2048
FFN_CHUNK = 256


def _ffn_chunks(tm):
    return (FFN_CHUNK,) * (tm // FFN_CHUNK - 1) + (FFN_CHUNK // 2,) * 2


def _ffn_front_body(x_ref, xp_ref, xn_ref, wg_ref, wu_ref, cwg_ref, cwu_ref,
                    cbg_ref, cbu_ref, o_ref, acc_ref, *, seq_len):
    tm = x_ref.shape[0]
    i = pl.program_id(0)
    sizes = _ffn_chunks(tm)
    starts = [sum(sizes[:c]) for c in range(len(sizes) + 1)]
    n_chunks = len(sizes)
    tn = wg_ref.shape[1]
    w = jnp.concatenate([wg_ref[...].astype(BF16), wu_ref[...].astype(BF16)], axis=1)

    def project(c):
        x = x_ref[starts[c]:starts[c + 1], :]
        lo, hi = HALO + starts[c], HALO + starts[c + 1]
        if c == 0:
            x, lo = jnp.concatenate([xp_ref[...], x], axis=0), 0
        if c == n_chunks - 1:
            x, hi = jnp.concatenate([x, xn_ref[...]], axis=0), tm + 2 * HALO
        acc_ref[lo:hi, :] = jnp.dot(x, w, preferred_element_type=F32)

    def conv(col0, cw_ref, cb_ref, c, at_start, at_end):
        n = sizes[c]
        r0 = HALO + starts[c]
        cols = slice(col0, col0 + tn)
        x_prev = acc_ref[r0 - 1:r0 - 1 + n, cols]
        x_next = acc_ref[r0 + 1:r0 + 1 + n, cols]
        row = lax.broadcasted_iota(jnp.int32, (SUB, 1), 0)
        head = jnp.where(jnp.logical_and(at_start, row == 0), 0.0, x_prev[:SUB])
        tail = jnp.where(jnp.logical_and(at_end, row == SUB - 1), 0.0, x_next[n - SUB:])
        x_prev = jnp.concatenate([head, x_prev[SUB:]], axis=0)
        x_next = jnp.concatenate([x_next[:n - SUB], tail], axis=0)
        return (cw_ref[0:1, :] * x_prev + cw_ref[1:2, :] * acc_ref[r0:r0 + n, cols]
                + cw_ref[2:3, :] * x_next + cb_ref[...])

    def gate(c):
        first = i * tm + starts[c]
        at_start = first % seq_len == 0
        at_end = (first + sizes[c]) % seq_len == 0
        g = conv(0, cwg_ref, cbg_ref, c, at_start, at_end)
        u = conv(tn, cwu_ref, cbu_ref, c, at_start, at_end)
        o_ref[starts[c]:starts[c + 1], :] = (g * jax.nn.sigmoid(g) * u).astype(o_ref.dtype)

    project(0)
    for c in range(n_chunks):
        if c + 1 < n_chunks:
            project(c + 1)
        gate(c)


def _ffn_front(h, w_up, conv_w, conv_b, layer, group):
    rows, d = h.shape
    two_f = w_up.shape[-1]
    d_ff = two_f // 2
    tm, tn = min(FFN_TM, rows), FFN_TN
    nj = d_ff // tn
    assert rows % tm == 0 and tm % FFN_CHUNK == 0 and d_ff % tn == 0
    assert group.seq_len % FFN_CHUNK == 0 and (tm % group.seq_len == 0 or group.seq_len % tm == 0)
    halo_blocks = rows // HALO

    def w(off):
        return pl.BlockSpec((None, d, tn), lambda i, j: (layer, 0, off + j))

    def cw(off):
        return pl.BlockSpec((None, 3, tn), lambda i, j: (layer, 0, off + j))

    def cb(off):
        return pl.BlockSpec((None, 1, tn), lambda i, j: (layer, 0, off + j))

    in_specs = [
        pl.BlockSpec((tm, d), lambda i, j: (i, 0), pipeline_mode=pl.Buffered(1)),
        pl.BlockSpec((HALO, d), lambda i, j: (jnp.maximum(i * (tm // HALO) - 1, 0), 0)),
        pl.BlockSpec((HALO, d), lambda i, j: (jnp.minimum((i + 1) * (tm // HALO), halo_blocks - 1), 0)),
        w(0), w(nj), cw(0), cw(nj), cb(0), cb(nj),
    ]
    cb3 = conv_b.reshape(conv_b.shape[0], 1, two_f)
    return pl.pallas_call(
        functools.partial(_ffn_front_body, seq_len=group.seq_len),
        grid=(rows // tm, nj),
        in_specs=in_specs,
        out_specs=pl.BlockSpec((tm, tn), lambda i, j: (i, j)),
        out_shape=jax.ShapeDtypeStruct((rows, d_ff), BF16),
        scratch_shapes=[pltpu.VMEM((tm + 2 * HALO, 2 * tn), F32)],
        compiler_params=_params(("parallel", "arbitrary")),
        name="ffn_front",
    )(h, h, h, w_up, w_up, conv_w, conv_w, cb3, cb3)


def kernel(x_prompt, x_sample, cache_a_k, cache_a_v, cache_b_k, cache_b_v, c, c_ctx,
           w_ada, b_ada, norm1_g, norm2_g,
           w_in_even, qn_a, kn_a, lam_q1, lam_k1, lam_q2, lam_k2, subln_g, qn_b, kn_b, rpb, w_out_even,
           w_in_odd, vnorm_g, w_sp, b_sp, w_out_odd,
           w_up, conv_w, conv_b, w_down):
    batch, seq, d = x_prompt.shape
    dec_batch, dec_seq, _ = x_sample.shape
    past_len = cache_a_k.shape[2]
    depth = w_ada.shape[0]

    groups = (_Group(batch * seq, seq, 0), _Group(dec_batch * dec_seq, dec_seq, 1))
    xs = [x_prompt.reshape(batch * seq, d), x_sample.reshape(dec_batch * dec_seq, d)]

    cond8 = jnp.concatenate([c_ctx[None], c, jnp.zeros((8 - 1 - dec_batch, d), F32)])

    w_down_b = w_down.astype(BF16)

    prompt_qkv = []
    for layer in range(depth):
        mod = _adaln(cond8, w_ada, b_ada, layer).reshape(8, 6, 1, d)
        if layer % 2 == 0:
            e = layer // 2
            lam_init = 0.8 - 0.6 * math.exp(-0.3 * layer)
            lam_vecs = jnp.stack([lam_q1[e], lam_k1[e], lam_q2[e], lam_k2[e]])
            ones = jnp.ones((SECTION,), F32)
            reps = SECTION // HEAD_DIM
            gains = jnp.concatenate([jnp.tile(qn_a[e] * Q_PRESCALE, reps), jnp.tile(kn_a[e], reps), ones,
                                     jnp.tile(qn_b[e] * Q_PRESCALE, reps), jnp.tile(kn_b[e], reps), ones])[None]
            sub_g = subln_g[e][None]
            mixed = []
            for gi, group in enumerate(groups):
                h = _norm_mod(xs[gi], norm1_g, layer, mod, 0, group)
                if gi == 0:
                    qkv = _qkv_proj(h, w_in_even, e, gains, group, rope=False, out_dtype=F32)
                    prompt_qkv.append(qkv)
                    oa, ob = _prompt_attn(qkv, lam_vecs, sub_g, lam_init, group.seq_len)
                else:
                    qkv = _qkv_proj(h, w_in_even, e, gains, group, rope=True, out_dtype=BF16)
                    ck_a = cache_a_k[:, e].reshape(dec_batch * past_len, SECTION)
                    cv_a = cache_a_v[:, e].reshape(dec_batch * past_len, SECTION)
                    ck_b = cache_b_k[:, e].reshape(dec_batch * past_len, SECTION)
                    cv_b = cache_b_v[:, e].reshape(dec_batch * past_len, SECTION)
                    oa = _latent_diff_attn(qkv, ck_a, cv_a, lam_vecs, sub_g, lam_init,
                                           group.seq_len, past_len)
                    ob = _latent_na_attn(qkv, ck_b, cv_b, rpb[e], group.seq_len, past_len)
                mixed.append((oa, ob))
            for gi, group in enumerate(groups):
                xs[gi] = _matmul_resid(mixed[gi], w_out_even, e, xs[gi], mod, 2, group,
                                       name="even_out_proj")
        else:
            o = layer // 2
            for gi, group in enumerate(groups):
                h = _norm_mod(xs[gi], norm1_g, layer, mod, 0, group)
                z = _matmul(h, w_in_odd, o, _gelu_epilogue, (), [], F32, name="odd_in_proj")
                gated = _spatial_gate(z, vnorm_g, w_sp, b_sp, o)
                xs[gi] = _matmul_resid(gated, w_out_odd, o, xs[gi], mod, 2, group,
                                       name="odd_out_proj")
        for gi, group in enumerate(groups):
            h = _norm_mod(xs[gi], norm2_g, layer, mod, 3, group)
            act = _ffn_front(h, w_up, conv_w, conv_b, layer, group)
            half = _matmul(act, w_down_b, layer, _plain_epilogue, (), [], F32,
                           k_parts=2, k_part=0, name="ffn_down_lo")
            xs[gi] = _matmul_resid(act, w_down_b, layer, xs[gi], mod, 5, group, partial_sum=half,
                                   k_parts=2, k_part=1, name="ffn_down_hi")

    def state(col0, shape):
        parts = [q[:, col0:col0 + SECTION].reshape((batch, seq) + shape) for q in prompt_qkv]
        return jnp.stack(parts, axis=1)

    h_a, h_b = N_HEADS_A, N_HEADS_B
    return (xs[0].reshape(batch, seq, d),
            xs[1].reshape(dec_batch, dec_seq, d),
            state(1 * SECTION, (2, h_a, HEAD_DIM)),
            state(2 * SECTION, (h_a, 2 * HEAD_DIM)),
            state(4 * SECTION, (h_b, HEAD_DIM)),
            state(5 * SECTION, (h_b, HEAD_DIM)))
```
